```python
import jax, jax.numpy as jnp
from jax import lax
import numpy as np

D_MODEL = 2048
BATCH = 2
SEQ = 8192
DEPTH = 1

GDN_QK_HEADS = 16
GDN_V_HEADS = 32
GDN_HEAD_DIM = 128
GDN_QK_WIDTH = GDN_QK_HEADS * GDN_HEAD_DIM
GDN_V_WIDTH = GDN_V_HEADS * GDN_HEAD_DIM
MLSTM_HEADS = 8
MLSTM_QK_DIM = D_MODEL // 16
MLSTM_V_DIM = D_MODEL // 8
MLSTM_QK_WIDTH = MLSTM_HEADS * MLSTM_QK_DIM
MLSTM_V_WIDTH = MLSTM_HEADS * MLSTM_V_DIM
CONV_WIDTH = 4
CHUNK = 64
NORM_EPS = 1e-6

IN_SPLITS = (
    2 * GDN_QK_WIDTH + GDN_V_WIDTH,
    GDN_V_HEADS,
    GDN_V_HEADS,
    GDN_V_WIDTH,
    2 * MLSTM_QK_WIDTH,
    MLSTM_V_WIDTH,
    MLSTM_HEADS,
    MLSTM_HEADS,
    MLSTM_V_WIDTH,
    MLSTM_V_WIDTH,
    D_MODEL,
    D_MODEL,
)
IN_WIDTH = sum(IN_SPLITS)

kernel_name = "hybrid_gdn_mlstm_gated_merge"


def rms_norm(x, w):
    xf = x.astype(jnp.float32)
    y = xf * lax.rsqrt(jnp.mean(xf * xf, axis=-1, keepdims=True) + NORM_EPS)
    return (y * w.astype(jnp.float32)).astype(x.dtype)


def l2_normalize(x):
    xf = x.astype(jnp.float32)
    return xf * lax.rsqrt(jnp.sum(xf * xf, axis=-1, keepdims=True) + NORM_EPS)


def causal_conv_silu(x, w):
    k_width, s = w.shape[0], x.shape[1]
    xp = jnp.pad(x, ((0, 0), (k_width - 1, 0), (0, 0)))
    y = sum(xp[:, j:j + s] * w[j] for j in range(k_width))
    return jax.nn.silu(y)


def to_chunks(x):
    b, s, h = x.shape[:3]
    x = x.reshape((b, s // CHUNK, CHUNK, h) + x.shape[3:])
    return jnp.moveaxis(x, (1, 3), (0, 2))


def from_chunks(x):
    x = jnp.moveaxis(x, (0, 2), (1, 3))
    b, nc, c, h, d = x.shape
    return x.reshape(b, nc * c, h, d)


def gated_delta_rule(q, k, v, g, beta):
    f32 = jnp.float32
    qc, kc, vc = (to_chunks(t.astype(f32)) for t in (q, k, v))
    gc, bc = to_chunks(g.astype(f32)), to_chunks(beta.astype(f32))
    dv = vc.shape[-1]
    causal = jnp.tril(jnp.ones((CHUNK, CHUNK), bool))
    strict = jnp.tril(jnp.ones((CHUNK, CHUNK), bool), -1)
    g_cum = jnp.cumsum(gc, axis=-1)
    decay = jnp.exp(jnp.where(causal, g_cum[..., :, None] - g_cum[..., None, :], -jnp.inf))
    k_beta = kc * bc[..., None]
    v_beta = vc * bc[..., None]
    lower = jnp.where(strict, jnp.einsum('nbhid,nbhjd->nbhij', k_beta, kc) * decay, 0.0)
    unit_lower = lower + jnp.eye(CHUNK, dtype=f32)
    rhs = jnp.concatenate([v_beta, k_beta * jnp.exp(g_cum)[..., None]], axis=-1)
    sol = lax.linalg.triangular_solve(unit_lower, rhs, left_side=True, lower=True, unit_diagonal=True)
    u, w = sol[..., :dv], sol[..., dv:]
    attn = jnp.einsum('nbhid,nbhjd->nbhij', qc, kc) * decay
    q_decay = qc * jnp.exp(g_cum)[..., None]
    k_tail = kc * jnp.exp(g_cum[..., -1:] - g_cum)[..., None]
    g_total = jnp.exp(g_cum[..., -1])

    def step(state, inp):
        attn_c, u_c, w_c, qd_c, kt_c, gt_c = inp
        v_new = u_c - jnp.einsum('bhck,bhkv->bhcv', w_c, state)
        o = jnp.einsum('bhck,bhkv->bhcv', qd_c, state) + jnp.einsum('bhij,bhjv->bhiv', attn_c, v_new)
        state = state * gt_c[..., None, None] + jnp.einsum('bhck,bhcv->bhkv', kt_c, v_new)
        return state, o

    b, h, dk = qc.shape[1], qc.shape[2], qc.shape[-1]
    s0 = jnp.zeros((b, h, dk, dv), f32)
    _, o = lax.scan(step, s0, (attn, u, w, q_decay, k_tail, g_total))
    return from_chunks(o)


def mlstm_chunkwise(q, k, v, i_pre, f_pre):
    f32 = jnp.float32
    qc, kc, vc = (to_chunks(t.astype(f32)) for t in (q, k, v))
    ic = to_chunks(i_pre.astype(f32))
    b_cum = jnp.cumsum(to_chunks(jax.nn.log_sigmoid(f_pre.astype(f32))), axis=-1)
    causal = jnp.tril(jnp.ones((CHUNK, CHUNK), bool))

    def step(carry, inp):
        c_state, n_state, m_state = carry
        q_c, k_c, v_c, i_c, b_c = inp
        log_d = jnp.where(causal, b_c[..., :, None] - b_c[..., None, :] + i_c[..., None, :], -jnp.inf)
        m_inter = b_c + m_state[..., None]
        m_t = jnp.maximum(m_inter, jnp.max(log_d, axis=-1))
        w_intra = jnp.exp(log_d - m_t[..., None])
        w_inter = jnp.exp(m_inter - m_t)
        s = jnp.einsum('bhid,bhjd->bhij', q_c, k_c) * w_intra
        num = (w_inter[..., None] * jnp.einsum('bhck,bhkv->bhcv', q_c, c_state)
               + jnp.einsum('bhij,bhjv->bhiv', s, v_c))
        den = w_inter * jnp.einsum('bhck,bhk->bhc', q_c, n_state) + jnp.sum(s, axis=-1)
        h = num / jnp.maximum(jnp.abs(den), jnp.exp(-m_t))[..., None]
        b_last = b_c[..., -1]
        log_end = b_last[..., None] - b_c + i_c
        m_new = jnp.maximum(b_last + m_state, jnp.max(log_end, axis=-1))
        wk = jnp.exp(log_end - m_new[..., None])
        carry_decay = jnp.exp(b_last + m_state - m_new)
        c_state = carry_decay[..., None, None] * c_state + jnp.einsum('bhc,bhck,bhcv->bhkv', wk, k_c, v_c)
        n_state = carry_decay[..., None] * n_state + jnp.einsum('bhc,bhck->bhk', wk, k_c)
        return (c_state, n_state, m_new), h

    b, h, dk, dv = qc.shape[1], qc.shape[2], qc.shape[-1], vc.shape[-1]
    init = (jnp.zeros((b, h, dk, dv), f32), jnp.zeros((b, h, dk), f32), jnp.zeros((b, h), f32))
    _, hs = lax.scan(step, init, (qc, kc, vc, ic, b_cum))
    return from_chunks(hs)


def hybrid_layer(x, c, w_ada, b_ada, norm_pre_w, w_in, gdn_conv_w, gdn_A_log, gdn_dt_bias, gdn_norm_w,
                 mlstm_conv_w, mlstm_b_i, mlstm_b_f, mlstm_norm_w, w_proj_gdn, w_proj_mlstm, w_out, norm_post_w):
    b, s, _ = x.shape
    shift, scale, gate = jnp.split(jax.nn.silu(c) @ w_ada + b_ada, 3, axis=-1)
    h = rms_norm(x, norm_pre_w) * (1.0 + scale[:, None]) + shift[:, None]
    proj = h @ w_in
    (gdn_qkv, gdn_a, gdn_b, gdn_z, ml_qk, ml_v, ml_i, ml_f, ml_o, ml_z, gate_a, gate_b) = jnp.split(
        proj, np.cumsum(IN_SPLITS)[:-1].tolist(), axis=-1)

    qkv = causal_conv_silu(gdn_qkv, gdn_conv_w)
    q_a, k_a, v_a = jnp.split(qkv, [GDN_QK_WIDTH, 2 * GDN_QK_WIDTH], axis=-1)
    rep = GDN_V_HEADS // GDN_QK_HEADS
    q_a = jnp.repeat(l2_normalize(q_a.reshape(b, s, GDN_QK_HEADS, GDN_HEAD_DIM)) * GDN_HEAD_DIM ** -0.5, rep, axis=2)
    k_a = jnp.repeat(l2_normalize(k_a.reshape(b, s, GDN_QK_HEADS, GDN_HEAD_DIM)), rep, axis=2)
    v_a = v_a.reshape(b, s, GDN_V_HEADS, GDN_HEAD_DIM)
    g_log = -jnp.exp(gdn_A_log.astype(jnp.float32)) * jax.nn.softplus(gdn_a.astype(jnp.float32) + gdn_dt_bias)
    beta = jax.nn.sigmoid(gdn_b.astype(jnp.float32))
    o_a = gated_delta_rule(q_a, k_a, v_a, g_log, beta)
    o_a = rms_norm(o_a, gdn_norm_w) * jax.nn.silu(gdn_z.astype(jnp.float32)).reshape(b, s, GDN_V_HEADS, GDN_HEAD_DIM)
    y_a = o_a.reshape(b, s, GDN_V_WIDTH).astype(x.dtype) @ w_proj_gdn

    qk = causal_conv_silu(ml_qk, mlstm_conv_w)
    q_b, k_b = jnp.split(qk, 2, axis=-1)
    q_b = q_b.reshape(b, s, MLSTM_HEADS, MLSTM_QK_DIM) * MLSTM_QK_DIM ** -0.5
    k_b = k_b.reshape(b, s, MLSTM_HEADS, MLSTM_QK_DIM)
    v_b = ml_v.reshape(b, s, MLSTM_HEADS, MLSTM_V_DIM)
    h_b = mlstm_chunkwise(q_b, k_b, v_b, ml_i + mlstm_b_i, ml_f + mlstm_b_f)
    h_b = rms_norm(h_b, mlstm_norm_w.reshape(MLSTM_HEADS, MLSTM_V_DIM)).reshape(b, s, MLSTM_V_WIDTH)
    h_b = jax.nn.sigmoid(ml_o) * h_b * jax.nn.silu(ml_z)
    y_b = h_b.astype(x.dtype) @ w_proj_mlstm

    merged = jax.nn.sigmoid(gate_a) * y_a + jax.nn.sigmoid(gate_b) * y_b
    out = merged @ w_out
    return x + gate[:, None] * rms_norm(out, norm_post_w)


def setup_inputs(seed: int = 0) -> dict:
    key = jax.random.key(seed)
    ks = jax.random.split(key, 20)
    d, f32 = D_MODEL, jnp.float32
    nrm = lambda k, shape, scale: jax.random.normal(k, shape, f32) * scale
    dt = jnp.exp(jax.random.uniform(ks[7], (DEPTH, GDN_V_HEADS), f32, np.log(1e-3), np.log(1e-1)))
    return {
        "x": nrm(ks[0], (BATCH, SEQ, d), 1.0),
        "c": nrm(ks[1], (BATCH, d), 1.0),
        "w_ada": nrm(ks[2], (DEPTH, d, 3 * d), d ** -0.5),
        "b_ada": nrm(ks[3], (DEPTH, 3 * d), 0.02),
        "norm_pre_w": 1.0 + nrm(ks[4], (DEPTH, d), 0.02),
        "w_in": nrm(ks[5], (DEPTH, d, IN_WIDTH), d ** -0.5),
        "gdn_conv_w": nrm(ks[6], (DEPTH, CONV_WIDTH, 2 * GDN_QK_WIDTH + GDN_V_WIDTH), CONV_WIDTH ** -0.5),
        "gdn_A_log": jnp.log(jax.random.uniform(ks[8], (DEPTH, GDN_V_HEADS), f32, 1.0, 16.0)),
        "gdn_dt_bias": dt + jnp.log(-jnp.expm1(-dt)),
        "gdn_norm_w": 1.0 + nrm(ks[9], (DEPTH, GDN_HEAD_DIM), 0.02),
        "mlstm_conv_w": nrm(ks[10], (DEPTH, CONV_WIDTH, 2 * MLSTM_QK_WIDTH), CONV_WIDTH ** -0.5),
        "mlstm_b_i": nrm(ks[11], (DEPTH, MLSTM_HEADS), 0.1),
        "mlstm_b_f": jnp.linspace(3.0, 6.0, MLSTM_HEADS, dtype=f32)[None] + nrm(ks[12], (DEPTH, MLSTM_HEADS), 0.1),
        "mlstm_norm_w": 1.0 + nrm(ks[13], (DEPTH, MLSTM_V_WIDTH), 0.02),
        "w_proj_gdn": nrm(ks[14], (DEPTH, GDN_V_WIDTH, d), GDN_V_WIDTH ** -0.5),
        "w_proj_mlstm": nrm(ks[15], (DEPTH, MLSTM_V_WIDTH, d), MLSTM_V_WIDTH ** -0.5),
        "w_out": nrm(ks[16], (DEPTH, d, d), d ** -0.5),
        "norm_post_w": 1.0 + nrm(ks[17], (DEPTH, d), 0.02),
    }


def reference(x, c, w_ada, b_ada, norm_pre_w, w_in, gdn_conv_w, gdn_A_log, gdn_dt_bias, gdn_norm_w,
              mlstm_conv_w, mlstm_b_i, mlstm_b_f, mlstm_norm_w, w_proj_gdn, w_proj_mlstm, w_out, norm_post_w):
    for l in range(DEPTH):
        x = hybrid_layer(x, c, w_ada[l], b_ada[l], norm_pre_w[l], w_in[l], gdn_conv_w[l], gdn_A_log[l],
                         gdn_dt_bias[l], gdn_norm_w[l], mlstm_conv_w[l], mlstm_b_i[l], mlstm_b_f[l],
                         mlstm_norm_w[l], w_proj_gdn[l], w_proj_mlstm[l], w_out[l], norm_post_w[l])
    return x
```

```python
import functools

import jax
import jax.numpy as jnp
import numpy as np
from jax import lax
from jax.experimental import pallas as pl
from jax.experimental.pallas import tpu as pltpu

F32 = jnp.float32
BF16 = jnp.bfloat16

NORM_EPS = 1e-6
CHUNK = 64
CONV_WIDTH = 4
CONV_HALO = 8
INV_BASE = 16

GDN_QK_HEADS = 16
GDN_V_HEADS = 32
GDN_HEAD_DIM = 128
MLSTM_HEADS = 8
MLSTM_QK_DIM = 128
MLSTM_V_DIM = 256
GATE_LANES = 128
GDN_GATE_SLOTS = GATE_LANES // GDN_QK_HEADS
MLSTM_GATE_SLOTS = GATE_LANES // MLSTM_HEADS

VMEM_LIMIT_BYTES = 56 * 1024 * 1024


def _cparams(n_axes):
    return pltpu.CompilerParams(dimension_semantics=("arbitrary",) * n_axes,
                                vmem_limit_bytes=VMEM_LIMIT_BYTES)


def _mm(a, b):
    return jnp.dot(a.astype(BF16), b.astype(BF16), preferred_element_type=F32)


def _mm_nt(a, b):
    return lax.dot_general(a.astype(BF16), b.astype(BF16), (((1,), (1,)), ((), ())),
                           preferred_element_type=F32)


def _mm_exact(a, b):
    return jnp.dot(a, b, precision=lax.Precision.HIGHEST, preferred_element_type=F32)


def _sigmoid(x):
    return 1.0 / (1.0 + jnp.exp(-x))


def _silu(x):
    return x * _sigmoid(x)


def _softplus(x):
    return jnp.maximum(x, 0.0) + jnp.log(1.0 + jnp.exp(-jnp.abs(x)))


def _rms(x):
    return x * lax.rsqrt(jnp.mean(x * x, axis=-1, keepdims=True) + NORM_EPS)


def _prenorm(x_ref, mod_ref, nw_ref, d):
    y = _rms(x_ref[...]) * nw_ref[...]
    return y * (1.0 + mod_ref[:, d:2 * d]) + mod_ref[:, 0:d]


def _ada_kernel(c_ref, w_ref, b_ref, o_ref):
    o_ref[...] = _mm(_silu(c_ref[...]), w_ref[...]) + b_ref[...]


def _ada(c8, w_ada, b_ada, tn=512):
    rows, d = c8.shape
    n = w_ada.shape[1]
    return pl.pallas_call(
        _ada_kernel,
        grid=(n // tn,),
        in_specs=[pl.BlockSpec((rows, d), lambda j: (0, 0)),
                  pl.BlockSpec((d, tn), lambda j: (0, j)),
                  pl.BlockSpec((1, tn), lambda j: (0, j))],
        out_specs=pl.BlockSpec((rows, tn), lambda j: (0, j)),
        out_shape=jax.ShapeDtypeStruct((rows, n), F32),
        compiler_params=_cparams(1),
    )(c8, w_ada, b_ada.reshape(1, n))


def _inproj_kernel(x_ref, mod_ref, nw_ref, w_ref, o_ref, h_ref, *, d):
    @pl.when(pl.program_id(1) == 0)
    def _():
        h_ref[...] = _prenorm(x_ref, mod_ref, nw_ref, d).astype(BF16)

    o_ref[...] = jnp.dot(h_ref[...], w_ref[...], preferred_element_type=F32).astype(BF16)


def _inproj(x2, mod, nw, w_big, seq, tm, tn):
    t, d = x2.shape
    n = w_big.shape[1]
    per_seq = seq // tm
    return pl.pallas_call(
        functools.partial(_inproj_kernel, d=d),
        grid=(t // tm, n // tn),
        in_specs=[pl.BlockSpec((tm, d), lambda i, j: (i, 0)),
                  pl.BlockSpec((None, 1, 3 * d), lambda i, j: (i // per_seq, 0, 0)),
                  pl.BlockSpec((1, d), lambda i, j: (0, 0)),
                  pl.BlockSpec((d, tn), lambda i, j: (0, j))],
        out_specs=pl.BlockSpec((tm, tn), lambda i, j: (i, j)),
        out_shape=jax.ShapeDtypeStruct((t, n), BF16),
        scratch_shapes=[pltpu.VMEM((tm, d), BF16)],
        compiler_params=_cparams(2),
    )(x2, mod, nw, w_big)


def _gates_kernel(x_ref, mod_ref, nw_ref, wg_ref, wm_ref, pg_ref, pm_ref,
                  gcol_ref, grow_ref, mcol_ref, mrow_ref, *, d, tg):
    h = _prenorm(x_ref, mod_ref, nw_ref, d)
    yg = _mm_exact(h, wg_ref[...])
    ym = _mm_exact(h, wm_ref[...])
    lane = lax.broadcasted_iota(jnp.int32, (tg, GATE_LANES), 1)

    slot = lane & (GDN_GATE_SLOTS - 1)
    g = -jnp.exp(pg_ref[0:1, :]) * _softplus(yg + pg_ref[1:2, :])
    is_g = (slot == 0) | (slot == 1) | (slot == 4) | (slot == 5)
    is_beta = (slot == 2) | (slot == 3)
    y_g = jnp.where(is_g, g, jnp.where(is_beta, _sigmoid(yg), 0.0))

    mslot = lane & (MLSTM_GATE_SLOTS - 1)
    pre = ym + pm_ref[0:1, :]
    y_m = jnp.where(mslot == 0, pre, jnp.where((mslot == 1) | (mslot == 2), -_softplus(-pre), 0.0))

    ri = lax.broadcasted_iota(jnp.int32, (tg, tg), 0)
    ci = lax.broadcasted_iota(jnp.int32, (tg, tg), 1)
    same = _same_block(ri, ci, CHUNK)
    lower = jnp.where(same & (ri >= ci), 1.0, 0.0)
    total = jnp.where(same, 1.0, 0.0)
    y_all = jnp.concatenate([y_g, y_m], axis=1)
    cum = _mm_exact(lower, y_all)
    tot = _mm_exact(total, y_all)

    g_out = jnp.where(slot < 2, cum[:, :GATE_LANES],
                      jnp.where((slot == 4) | (slot == 5), tot[:, :GATE_LANES], y_g))
    m_out = jnp.where(mslot == 1, cum[:, GATE_LANES:],
                      jnp.where(mslot == 2, tot[:, GATE_LANES:], y_m))
    gcol_ref[...] = g_out
    grow_ref[...] = g_out.T
    mcol_ref[...] = m_out
    mrow_ref[...] = m_out.T


def _gates(x2, mod, nw, wg, wm, pg, pm, batch, seq, tg):
    t, d = x2.shape
    per_seq = seq // tg
    col = jax.ShapeDtypeStruct((t, GATE_LANES), F32)
    row = jax.ShapeDtypeStruct((batch, GATE_LANES, seq), F32)
    col_spec = pl.BlockSpec((tg, GATE_LANES), lambda i: (i, 0))
    row_spec = pl.BlockSpec((None, GATE_LANES, tg), lambda i: (i // per_seq, 0, i % per_seq))
    const = lambda shape: pl.BlockSpec(shape, lambda i: (0, 0))
    return pl.pallas_call(
        functools.partial(_gates_kernel, d=d, tg=tg),
        grid=(t // tg,),
        in_specs=[pl.BlockSpec((tg, d), lambda i: (i, 0)),
                  pl.BlockSpec((None, 1, 3 * d), lambda i: (i // per_seq, 0, 0)),
                  const((1, d)), const((d, GATE_LANES)), const((d, GATE_LANES)),
                  const((2, GATE_LANES)), const((1, GATE_LANES))],
        out_specs=[col_spec, row_spec, col_spec, row_spec],
        out_shape=[col, row, col, row],
        compiler_params=_cparams(1),
    )(x2, mod, nw, wg, wm, pg, pm)


def _conv_silu(x_ref, tail_ref, xx_ref, w_ref, tt):
    x = x_ref[...].astype(F32)
    xx_ref[0:CONV_HALO, :] = tail_ref[...]
    xx_ref[CONV_HALO:CONV_HALO + tt, :] = x
    tail_ref[...] = x[tt - CONV_HALO:tt, :]
    w = w_ref[...]
    y = w[CONV_WIDTH - 1:CONV_WIDTH, :] * x
    for j in range(CONV_WIDTH - 1):
        back = CONV_WIDTH - 1 - j
        y = y + w[j:j + 1, :] * xx_ref[CONV_HALO - back:CONV_HALO - back + tt, :]
    return _silu(y)


def _same_block(ri, ci, size):
    shift = size.bit_length() - 1
    return (ri >> shift) == (ci >> shift)


def _unit_lower_inverse(a, ri, ci):
    diag = _same_block(ri, ci, INV_BASE)
    ab = jnp.where(diag, a, 0.0)
    inv = jnp.where(ri == ci, 1.0, 0.0) - ab
    order = 2
    while order < INV_BASE:
        ab = _mm(ab, ab)
        inv = inv + _mm(inv, ab)
        order *= 2
    size = INV_BASE
    while size < CHUNK:
        merged = _same_block(ri, ci, 2 * size)
        off = jnp.where(merged & jnp.logical_not(diag), a, 0.0)
        inv = inv - _mm(inv, _mm(off, inv))
        diag, size = merged, 2 * size
    return inv


def _lane_slot_shift(group, slots):
    return lax.rem((GATE_LANES // slots - group) * slots, GATE_LANES)


def _gdn_kernel(q_ref, k_ref, v_ref, z_ref, gcol_ref, grow_ref, wq_ref, wk_ref, wv_ref, nw_ref,
                o_ref, s_ref, tq_ref, tk_ref, tv_ref, xq_ref, xk_ref, xv_ref, *, tt):
    hd = GDN_HEAD_DIM

    @pl.when(pl.program_id(2) == 0)
    def _():
        s_ref[...] = jnp.zeros_like(s_ref)
        tq_ref[...] = jnp.zeros_like(tq_ref)
        tk_ref[...] = jnp.zeros_like(tk_ref)
        tv_ref[...] = jnp.zeros_like(tv_ref)

    q = _conv_silu(q_ref, tq_ref, xq_ref, wq_ref, tt)
    k = _conv_silu(k_ref, tk_ref, xk_ref, wk_ref, tt)
    v = _conv_silu(v_ref, tv_ref, xv_ref, wv_ref, tt)
    q = q * lax.rsqrt(jnp.sum(q * q, axis=-1, keepdims=True) + NORM_EPS) * (hd ** -0.5)
    k = k * lax.rsqrt(jnp.sum(k * k, axis=-1, keepdims=True) + NORM_EPS)
    k16 = k.astype(BF16)
    kk = _mm_nt(k16, k16)
    qk = _mm_nt(q, k16)
    k_t = k.T
    ri = lax.broadcasted_iota(jnp.int32, (tt, tt), 0)
    ci = lax.broadcasted_iota(jnp.int32, (tt, tt), 1)
    same = _same_block(ri, ci, CHUNK)
    causal, strict = same & (ri >= ci), same & (ri > ci)

    gcol = pltpu.roll(gcol_ref[...], _lane_slot_shift(pl.program_id(1), GDN_GATE_SLOTS), axis=1)
    grow = grow_ref[...]
    for hh in range(2):
        gc_c, be_c = gcol[:, hh:hh + 1], gcol[:, 2 + hh:3 + hh]
        gc_r, be_r, gl_r = grow[hh:hh + 1, :], grow[2 + hh:3 + hh, :], grow[4 + hh:5 + hh, :]
        decay = jnp.exp(jnp.where(causal, gc_c - gc_r, -jnp.inf))
        a = jnp.where(strict, be_c * kk * decay, 0.0)
        t_beta = _unit_lower_inverse(a, ri, ci) * be_r
        u = _mm(t_beta, v[:, hh * hd:(hh + 1) * hd])
        w = _mm(t_beta * jnp.exp(gc_r), k16)
        attn = qk * decay
        qd = q * jnp.exp(gc_c)
        kt_t = k_t * jnp.exp(gl_r - gc_r)
        g_tot = jnp.exp(gl_r)
        state = s_ref[hh]
        outs = []
        for c in range(tt // CHUNK):
            r0, r1 = c * CHUNK, (c + 1) * CHUNK
            ws = _mm(jnp.concatenate([w[r0:r1], qd[r0:r1]], axis=0), state)
            v_new = u[r0:r1] - ws[0:CHUNK]
            outs.append(ws[CHUNK:2 * CHUNK] + _mm(attn[r0:r1, r0:r1], v_new))
            state = state * g_tot[:, r0:r0 + 1] + _mm(kt_t[:, r0:r1], v_new)
        s_ref[hh] = state
        o = _rms(jnp.concatenate(outs, axis=0)) * nw_ref[...]
        z = z_ref[:, hh * hd:(hh + 1) * hd].astype(F32)
        o_ref[:, hh * hd:(hh + 1) * hd] = (o * _silu(z)).astype(BF16)


def _gdn(p, gcol, grow, conv_w, norm_w, batch, seq, tt, col_q, col_k, col_v, col_z):
    t = batch * seq
    hd = GDN_HEAD_DIM
    per_seq = seq // tt
    rowi = lambda b, h, i: b * per_seq + i
    return pl.pallas_call(
        functools.partial(_gdn_kernel, tt=tt),
        grid=(batch, GDN_QK_HEADS, per_seq),
        in_specs=[
            pl.BlockSpec((tt, hd), lambda b, h, i: (rowi(b, h, i), col_q // hd + h)),
            pl.BlockSpec((tt, hd), lambda b, h, i: (rowi(b, h, i), col_k // hd + h)),
            pl.BlockSpec((tt, 2 * hd), lambda b, h, i: (rowi(b, h, i), col_v // (2 * hd) + h)),
            pl.BlockSpec((tt, 2 * hd), lambda b, h, i: (rowi(b, h, i), col_z // (2 * hd) + h)),
            pl.BlockSpec((tt, GATE_LANES), lambda b, h, i: (rowi(b, h, i), 0)),
            pl.BlockSpec((None, None, GDN_GATE_SLOTS, tt), lambda b, h, i: (b, h, 0, i)),
            pl.BlockSpec((CONV_WIDTH, hd), lambda b, h, i: (0, h)),
            pl.BlockSpec((CONV_WIDTH, hd), lambda b, h, i: (0, GDN_QK_HEADS + h)),
            pl.BlockSpec((CONV_WIDTH, 2 * hd), lambda b, h, i: (0, GDN_QK_HEADS + h)),
            pl.BlockSpec((1, hd), lambda b, h, i: (0, 0)),
        ],
        out_specs=pl.BlockSpec((tt, 2 * hd), lambda b, h, i: (rowi(b, h, i), h)),
        out_shape=jax.ShapeDtypeStruct((t, GDN_V_HEADS * hd), BF16),
        scratch_shapes=[pltpu.VMEM((2, hd, hd), F32),
                        pltpu.VMEM((CONV_HALO, hd), F32), pltpu.VMEM((CONV_HALO, hd), F32),
                        pltpu.VMEM((CONV_HALO, 2 * hd), F32),
                        pltpu.VMEM((CONV_HALO + tt, hd), F32), pltpu.VMEM((CONV_HALO + tt, hd), F32),
                        pltpu.VMEM((CONV_HALO + tt, 2 * hd), F32)],
        compiler_params=_cparams(3),
    )(p, p, p, p, gcol, grow, conv_w, conv_w, conv_w, norm_w)


def _mlstm_kernel(q_ref, k_ref, v_ref, og_ref, z_ref, mcol_ref, mrow_ref, wq_ref, wk_ref, nw_ref,
                  o_ref, c_ref, m_ref, tq_ref, tk_ref, xq_ref, xk_ref, *, tt):
    dk, dv = MLSTM_QK_DIM, MLSTM_V_DIM
    n_chunks = tt // CHUNK

    @pl.when(pl.program_id(2) == 0)
    def _():
        c_ref[...] = jnp.zeros_like(c_ref)
        m_ref[...] = jnp.zeros_like(m_ref)
        tq_ref[...] = jnp.zeros_like(tq_ref)
        tk_ref[...] = jnp.zeros_like(tk_ref)

    q = _conv_silu(q_ref, tq_ref, xq_ref, wq_ref, tt) * (dk ** -0.5)
    k = _conv_silu(k_ref, tk_ref, xk_ref, wk_ref, tt)
    q16 = q.astype(BF16)
    lane = lax.broadcasted_iota(jnp.int32, (tt, 128), 1)
    v_aug = jnp.concatenate([v_ref[...], jnp.where(lane == 0, 1.0, 0.0).astype(BF16)], axis=1)
    qk = _mm_nt(q16, k)
    ri = lax.broadcasted_iota(jnp.int32, (tt, tt), 0)
    ci = lax.broadcasted_iota(jnp.int32, (tt, tt), 1)
    causal = _same_block(ri, ci, CHUNK) & (ri >= ci)

    mcol =pltpu.roll(mcol_ref[...], _lane_slot_shift(pl.program_id(1), MLSTM_GATE_SLOTS), axis=1)
    mrow = mrow_ref[...]
    b_c = mcol[:, 1:2]
    i_r, b_r, bl_r = mrow[0:1, :], mrow[1:2, :], mrow[2:3, :]
    log_end = bl_r - b_r + i_r

    m = m_ref[0:1, 0:1]
    chunk_shift = CHUNK.bit_length() - 1
    row_chunk = lax.broadcasted_iota(jnp.int32, (tt, 1), 0) >> chunk_shift
    lane_chunk = lax.broadcasted_iota(jnp.int32, (1, tt), 1) >> chunk_shift
    m_start_c = jnp.zeros((tt, 1), F32)
    m_start_r = jnp.zeros((1, tt), F32)
    m_next_r = jnp.zeros((1, tt), F32)
    for c in range(n_chunks):
        r0, r1 = c * CHUNK, (c + 1) * CHUNK
        m_next = jnp.maximum(bl_r[:, r0:r0 + 1] + m, jnp.max(log_end[:, r0:r1], axis=1, keepdims=True))
        m_start_c = jnp.where(row_chunk == c, m, m_start_c)
        m_start_r = jnp.where(lane_chunk == c, m, m_start_r)
        m_next_r = jnp.where(lane_chunk == c, m_next, m_next_r)
        m = m_next
    m_ref[...] = jnp.broadcast_to(m, m_ref.shape)

    m_inter = b_c + m_start_c
    log_d = jnp.where(causal, b_c - b_r + i_r, -jnp.inf)
    m_t = jnp.maximum(m_inter, jnp.max(log_d, axis=1, keepdims=True))
    s = qk * jnp.exp(log_d - m_t)
    intra = _mm(s, v_aug)
    w_inter = jnp.exp(m_inter - m_t)
    k_ts = k.T * jnp.exp(log_end - m_next_r)
    carry_decay = jnp.exp(bl_r + m_start_r - m_next_r)

    state = c_ref[...]
    inters = []
    for c in range(n_chunks):
        r0, r1 = c * CHUNK, (c + 1) * CHUNK
        inters.append(_mm(q16[r0:r1], state))
        state = state * carry_decay[:, r0:r0 + 1] + _mm(k_ts[:, r0:r1], v_aug[r0:r1])
    c_ref[...] = state

    num = w_inter * jnp.concatenate(inters, axis=0) + intra
    den = jnp.maximum(jnp.abs(num[:, dv:dv + 1]), jnp.exp(-m_t))
    h = _rms(num[:, 0:dv] / den) * nw_ref[...]
    o_ref[...] = (_sigmoid(og_ref[...].astype(F32)) * h * _silu(z_ref[...].astype(F32))).astype(BF16)


def _mlstm(p, mcol, mrow, conv_w, norm_w, batch, seq, tt, col_q, col_k, col_v, col_o, col_z):
    t = batch * seq
    dk, dv = MLSTM_QK_DIM, MLSTM_V_DIM
    per_seq = seq // tt
    rowi = lambda b, h, i: b * per_seq + i
    return pl.pallas_call(
        functools.partial(_mlstm_kernel, tt=tt),
        grid=(batch, MLSTM_HEADS, per_seq),
        in_specs=[
            pl.BlockSpec((tt, dk), lambda b, h, i: (rowi(b, h, i), col_q // dk + h)),
            pl.BlockSpec((tt, dk), lambda b, h, i: (rowi(b, h, i), col_k // dk + h)),
            pl.BlockSpec((tt, dv), lambda b, h, i: (rowi(b, h, i), col_v // dv + h)),
            pl.BlockSpec((tt, dv), lambda b, h, i: (rowi(b, h, i), col_o // dv + h)),
            pl.BlockSpec((tt, dv), lambda b, h, i: (rowi(b, h, i), col_z // dv + h)),
            pl.BlockSpec((tt, GATE_LANES), lambda b, h, i: (rowi(b, h, i), 0)),
            pl.BlockSpec((None, None, MLSTM_GATE_SLOTS, tt), lambda b, h, i: (b, h, 0, i)),
            pl.BlockSpec((CONV_WIDTH, dk), lambda b, h, i: (0, h)),
            pl.BlockSpec((CONV_WIDTH, dk), lambda b, h, i: (0, MLSTM_HEADS + h)),
            pl.BlockSpec((1, dv), lambda b, h, i: (0, h)),
        ],
        out_specs=pl.BlockSpec((tt, dv), lambda b, h, i: (rowi(b, h, i), h)),
        out_shape=jax.ShapeDtypeStruct((t, MLSTM_HEADS * dv), BF16),
        scratch_shapes=[pltpu.VMEM((dk, dv + 128), F32), pltpu.VMEM((8, 128), F32),
                        pltpu.VMEM((CONV_HALO, dk), F32), pltpu.VMEM((CONV_HALO, dk), F32),
                        pltpu.VMEM((CONV_HALO + tt, dk), F32), pltpu.VMEM((CONV_HALO + tt, dk), F32)],
        compiler_params=_cparams(3),
    )(p, p, p, p, p, mcol, mrow, conv_w, conv_w, norm_w)


def _merge_kernel(ya_ref, hb_ref, wa_ref, wb_ref, ga_ref, gb_ref, o_ref):
    ya = jnp.dot(ya_ref[...], wa_ref[...], preferred_element_type=F32)
    yb = jnp.dot(hb_ref[...], wb_ref[...], preferred_element_type=F32)
    merged = _sigmoid(ga_ref[...].astype(F32)) * ya + _sigmoid(gb_ref[...].astype(F32)) * yb
    o_ref[...] = merged.astype(BF16)


def _merge(ya, hb, wa, wb, p, col_ga, col_gb, tm, tn):
    t = ya.shape[0]
    d = wa.shape[1]
    return pl.pallas_call(
        _merge_kernel,
        grid=(t // tm, d // tn),
        in_specs=[pl.BlockSpec((tm, ya.shape[1]), lambda i, j: (i, 0)),
                  pl.BlockSpec((tm, hb.shape[1]), lambda i, j: (i, 0)),
                  pl.BlockSpec((wa.shape[0], tn), lambda i, j: (0, j)),
                  pl.BlockSpec((wb.shape[0], tn), lambda i, j: (0, j)),
                  pl.BlockSpec((tm, tn), lambda i, j: (i, col_ga // tn + j)),
                  pl.BlockSpec((tm, tn), lambda i, j: (i, col_gb // tn + j))],
        out_specs=pl.BlockSpec((tm, tn), lambda i, j: (i, j)),
        out_shape=jax.ShapeDtypeStruct((t, d), BF16),
        compiler_params=_cparams(2),
    )(ya, hb, wa, wb, p, p)


def _out_kernel(mg_ref, w_ref, x_ref, mod_ref, nw_ref, o_ref, *, d):
    out = jnp.dot(mg_ref[...], w_ref[...], preferred_element_type=F32)
    o_ref[...] = x_ref[...] + mod_ref[:, 2 * d:3 * d] * (_rms(out) * nw_ref[...])


def _out(mg, w_out, x2, mod, nw, seq, tm):
    t, d = x2.shape
    per_seq = seq // tm
    return pl.pallas_call(
        functools.partial(_out_kernel, d=d),
        grid=(t // tm,),
        in_specs=[pl.BlockSpec((tm, d), lambda i: (i, 0)),
                  pl.BlockSpec((d, d), lambda i: (0, 0)),
                  pl.BlockSpec((tm, d), lambda i: (i, 0)),
                  pl.BlockSpec((None, 1, 3 * d), lambda i: (i // per_seq, 0, 0)),
                  pl.BlockSpec((1, d), lambda i: (0, 0))],
        out_specs=pl.BlockSpec((tm, d), lambda i: (i, 0)),
        out_shape=jax.ShapeDtypeStruct((t, d), F32),
        compiler_params=_cparams(1),
    )(mg, w_out, x2, mod, nw)


def _gate_columns(d):
    qkv = 2 * GDN_QK_HEADS * GDN_HEAD_DIM + GDN_V_HEADS * GDN_HEAD_DIM
    col_a, col_b = qkv, qkv + GDN_V_HEADS
    col_i = col_b + GDN_V_HEADS + GDN_V_HEADS * GDN_HEAD_DIM + 2 * MLSTM_HEADS * MLSTM_QK_DIM + MLSTM_HEADS * MLSTM_V_DIM
    col_f = col_i + MLSTM_HEADS
    g_src, g_head, g_use = np.zeros(GATE_LANES, np.int32), np.zeros(GATE_LANES, np.int32), np.zeros(GATE_LANES, bool)
    m_src, m_head, m_use = np.zeros(GATE_LANES, np.int32), np.zeros(GATE_LANES, np.int32), np.zeros(GATE_LANES, bool)
    for lane in range(GATE_LANES):
        pair, slot = divmod(lane, GDN_GATE_SLOTS)
        if slot < 6:
            head = 2 * pair + slot % 2
            g_src[lane] = (col_b if slot in (2, 3) else col_a) + head
            g_head[lane], g_use[lane] = head, True
        head, slot = divmod(lane, MLSTM_GATE_SLOTS)
        if slot < 3:
            m_src[lane] = (col_i if slot == 0 else col_f) + head
            m_head[lane], m_use[lane] = head, True
    slot_is_i = (np.arange(GATE_LANES) % MLSTM_GATE_SLOTS) == 0
    return (g_src, g_head, g_use), (m_src, m_head, m_use), slot_is_i


def _layer(x, c, w_ada, b_ada, norm_pre_w, w_in, gdn_conv_w, gdn_A_log, gdn_dt_bias, gdn_norm_w,
           mlstm_conv_w, mlstm_b_i, mlstm_b_f, mlstm_norm_w, w_proj_gdn, w_proj_mlstm, w_out, norm_post_w):
    batch, seq, d = x.shape
    t = batch * seq
    qk_w = GDN_QK_HEADS * GDN_HEAD_DIM
    v_w = GDN_V_HEADS * GDN_HEAD_DIM
    mqk_w = MLSTM_HEADS * MLSTM_QK_DIM
    mv_w = MLSTM_HEADS * MLSTM_V_DIM
    assert d == qk_w == mv_w and seq % 256 == 0

    src_a = 2 * qk_w + v_w
    src_z = src_a + 2 * GDN_V_HEADS
    src_mqk = src_z + v_w
    src_i = src_mqk + 2 * mqk_w + mv_w
    src_o = src_i + 2 * MLSTM_HEADS
    col_q, col_k, col_v, col_z = 0, qk_w, 2 * qk_w, 2 * qk_w + v_w
    col_mq = col_z + v_w
    col_mk, col_mv = col_mq + mqk_w, col_mq + 2 * mqk_w
    col_mo = col_mv + mv_w
    col_mz, col_ga, col_gb = col_mo + mv_w, col_mo + 2 * mv_w, col_mo + 2 * mv_w + d
    w_big = jnp.concatenate([w_in[:, :src_a], w_in[:, src_z:src_i], w_in[:, src_o:]], axis=1).astype(BF16)
    assert w_big.shape[1] == col_gb + d

    (g_src, g_head, g_use), (m_src, m_head, m_use), slot_is_i = _gate_columns(d)
    wg = jnp.where(g_use[None, :], jnp.take(w_in, g_src, axis=1), 0.0)
    wm = jnp.where(m_use[None, :], jnp.take(w_in, m_src, axis=1), 0.0)
    pg = jnp.stack([jnp.where(g_use, jnp.take(gdn_A_log, g_head), 0.0),
                    jnp.where(g_use, jnp.take(gdn_dt_bias, g_head), 0.0)])
    pm = jnp.where(m_use, jnp.where(slot_is_i, jnp.take(mlstm_b_i, m_head), jnp.take(mlstm_b_f, m_head)), 0.0)[None, :]

    x2 = x.reshape(t, d)
    rows = max(8, batch)
    c8 = jnp.pad(c, ((0, rows - batch), (0, 0)))
    mod = _ada(c8, w_ada, b_ada)[:batch].reshape(batch, 1, 3 * d)
    nw_pre = norm_pre_w.reshape(1, d)

    tm = min(1024, seq)
    p = _inproj(x2, mod, nw_pre, w_big, seq, tm=tm, tn=1024)
    gcol, grow, mcol, mrow = _gates(x2, mod, nw_pre, wg, wm, pg, pm, batch, seq, tg=min(512, seq))
    grow = grow.reshape(batch, GDN_QK_HEADS, GDN_GATE_SLOTS, seq)
    mrow = mrow.reshape(batch, MLSTM_HEADS, MLSTM_GATE_SLOTS, seq)

    tt = 256
    ya = _gdn(p, gcol, grow, gdn_conv_w, gdn_norm_w.reshape(1, GDN_HEAD_DIM), batch, seq, tt,
              col_q, col_k, col_v, col_z)
    hb = _mlstm(p, mcol, mrow, mlstm_conv_w, mlstm_norm_w.reshape(1, mv_w), batch, seq, tt,
                col_mq, col_mk, col_mv, col_mo, col_mz)
    mg = _merge(ya, hb, w_proj_gdn.astype(BF16), w_proj_mlstm.astype(BF16), p, col_ga, col_gb, tm=tm, tn=512)
    y = _out(mg, w_out.astype(BF16), x2, mod, norm_post_w.reshape(1, d), seq, tm=min(512, seq))
    return y.reshape(batch, seq, d)


def kernel(x, c, w_ada, b_ada, norm_pre_w, w_in, gdn_conv_w, gdn_A_log, gdn_dt_bias, gdn_norm_w, mlstm_conv_w, mlstm_b_i, mlstm_b_f, mlstm_norm_w, w_proj_gdn, w_proj_mlstm, w_out, norm_post_w):
    for l in range(w_ada.shape[0]):
        x = _layer(x, c, w_ada[l], b_ada[l], norm_pre_w[l], w_in[l], gdn_conv_w[l], gdn_A_log[l],
                   gdn_dt_bias[l], gdn_norm_w[l], mlstm_conv_w[l], mlstm_b_i[l], mlstm_b_f[l],
                   mlstm_norm_w[l], w_proj_gdn[l], w_proj_mlstm[l], w_out[l], norm_post_w[l])
    return x
```

```python
import functools

import jax
import jax.numpy as jnp
import numpy as np
from jax import lax
from jax.experimental import pallas as pl
from jax.experimental.pallas import tpu as pltpu

F32 = jnp.float32
BF16 = jnp.bfloat16

NORM_EPS = 1e-6
CHUNK = 64
CONV_WIDTH = 4
CONV_HALO = 8
INV_BASE = 16
GROUP = 128

GDN_QK_HEADS = 16
GDN_V_HEADS = 32
GDN_HEAD_DIM = 128
MLSTM_HEADS = 8
MLSTM_QK_DIM = 128
MLSTM_V_DIM = 256
GATE_LANES = 128
GDN_GATE_SLOTS = GATE_LANES // GDN_QK_HEADS
MLSTM_GATE_SLOTS = GATE_LANES // MLSTM_HEADS

VMEM_LIMIT_BYTES = 56 * 1024 * 1024


def _cparams(n_axes):
    return pltpu.CompilerParams(dimension_semantics=("arbitrary",) * n_axes,
                                vmem_limit_bytes=VMEM_LIMIT_BYTES)


def _mm(a, b):
    return jnp.dot(a.astype(BF16), b.astype(BF16), preferred_element_type=F32)


def _mm_nt(a, b):
    return lax.dot_general(a.astype(BF16), b.astype(BF16), (((1,), (1,)), ((), ())),
                           preferred_element_type=F32)


def _mm_exact(a, b):
    return jnp.dot(a, b, precision=lax.Precision.HIGHEST, preferred_element_type=F32)


def _sigmoid(x):
    return 1.0 / (1.0 + jnp.exp(-x))


def _silu(x):
    return x * _sigmoid(x)


def _softplus(x):
    return jnp.maximum(x, 0.0) + jnp.log(1.0 + jnp.exp(-jnp.abs(x)))


def _rms(x):
    return x * lax.rsqrt(jnp.mean(x * x, axis=-1, keepdims=True) + NORM_EPS)


def _prenorm(x_ref, mod_ref, nw_ref, d):
    y = _rms(x_ref[...]) * nw_ref[...]
    return y * (1.0 + mod_ref[:, d:2 * d]) + mod_ref[:, 0:d]


def _ada_kernel(c_ref, w_ref, b_ref, o_ref):
    o_ref[...] = _mm(_silu(c_ref[...]), w_ref[...]) + b_ref[...]


def _ada(c8, w_ada, b_ada, tn=512):
    rows, d = c8.shape
    n = w_ada.shape[1]
    return pl.pallas_call(
        _ada_kernel,
        grid=(n // tn,),
        in_specs=[pl.BlockSpec((rows, d), lambda j: (0, 0)),
                  pl.BlockSpec((d, tn), lambda j: (0, j)),
                  pl.BlockSpec((1, tn), lambda j: (0, j))],
        out_specs=pl.BlockSpec((rows, tn), lambda j: (0, j)),
        out_shape=jax.ShapeDtypeStruct((rows, n), F32),
        compiler_params=_cparams(1),
    )(c8, w_ada, b_ada.reshape(1, n))


def _inproj_kernel(x_ref, mod_ref, nw_ref, w_ref, cw_ref, o_ref, h_ref, tail_ref, *acc_refs,
                   d, tm, sub, per_seq, conv_tiles):
    i, j = pl.program_id(0), pl.program_id(1)

    @pl.when(j == 0)
    def _():
        h_ref[...] = _prenorm(x_ref, mod_ref, nw_ref, d).astype(BF16)

    is_conv = functools.reduce(jnp.logical_or, [(j >= lo) & (j < hi) for lo, hi in conv_tiles])

    @pl.when(jnp.logical_not(is_conv))
    def _():
        for r in range(tm // sub):
            o_ref[r * sub:(r + 1) * sub, :] = jnp.dot(
                h_ref[r * sub:(r + 1) * sub, :], w_ref[...], preferred_element_type=F32).astype(BF16)

    @pl.when(is_conv)
    def _():
        first = (i % per_seq) == 0

        @pl.when(first)
        def _():
            acc_refs[0][0:CONV_HALO, :] = jnp.zeros((CONV_HALO, w_ref.shape[1]), F32)

        @pl.when(jnp.logical_not(first))
        def _():
            acc_refs[0][0:CONV_HALO, :] = tail_ref[j]

        cw = cw_ref[...]
        for r, acc_ref in enumerate(acc_refs):
            acc = jnp.dot(h_ref[r * sub:(r + 1) * sub, :], w_ref[...], preferred_element_type=F32)
            acc_ref[CONV_HALO:CONV_HALO + sub, :] = acc
            if r + 1 < len(acc_refs):
                acc_refs[r + 1][0:CONV_HALO, :] = acc[sub - CONV_HALO:sub, :]
            else:
                tail_ref[j] = acc[sub - CONV_HALO:sub, :]
            y = cw[CONV_WIDTH - 1:CONV_WIDTH, :] * acc
            for tap in range(CONV_WIDTH - 1):
                back = CONV_WIDTH - 1 - tap
                y = y + cw[tap:tap + 1, :] * acc_ref[CONV_HALO - back:CONV_HALO - back + sub, :]
            o_ref[r * sub:(r + 1) * sub, :] = (_silu(y) * cw[CONV_WIDTH:CONV_WIDTH + 1, :]).astype(BF16)


def _inproj(x2, mod, nw, w_big, conv_big, seq, tm, tn, conv_tiles):
    t, d = x2.shape
    n = w_big.shape[1]
    per_seq = seq // tm
    sub = min(256, tm)
    return pl.pallas_call(
        functools.partial(_inproj_kernel, d=d, tm=tm, sub=sub, per_seq=per_seq, conv_tiles=conv_tiles),
        grid=(t // tm, n // tn),
        in_specs=[pl.BlockSpec((tm, d), lambda i, j: (i, 0)),
                  pl.BlockSpec((None, 1, 3 * d), lambda i, j: (i // per_seq, 0, 0)),
                  pl.BlockSpec((1, d), lambda i, j: (0, 0)),
                  pl.BlockSpec((d, tn), lambda i, j: (0, j)),
                  pl.BlockSpec((8, tn), lambda i, j: (0, j))],
        out_specs=pl.BlockSpec((tm, tn), lambda i, j: (i, j)),
        out_shape=jax.ShapeDtypeStruct((t, n), BF16),
        scratch_shapes=[pltpu.VMEM((tm, d), BF16), pltpu.VMEM((n // tn, CONV_HALO, tn), F32)]
        + [pltpu.VMEM((CONV_HALO + sub, tn), F32)] * (tm // sub),
        compiler_params=_cparams(2),
    )(x2, mod, nw, w_big, conv_big)


def _same_block(ri, ci, size):
    shift = size.bit_length() - 1
    return (ri >> shift) == (ci >> shift)


def _gates_kernel(x_ref, mod_ref, nw_ref, wg_ref, wm_ref, pg_ref, pm_ref,
                  gcol_ref, grow_ref, mcol_ref, mrow_ref, *, d, tg):
    h = _prenorm(x_ref, mod_ref, nw_ref, d)
    yg = _mm_exact(h, wg_ref[...])
    ym = _mm_exact(h, wm_ref[...])
    lane = lax.broadcasted_iota(jnp.int32, (tg, GATE_LANES), 1)

    slot = lane & (GDN_GATE_SLOTS - 1)
    g = -jnp.exp(pg_ref[0:1, :]) * _softplus(yg + pg_ref[1:2, :])
    is_g = (slot == 0) | (slot == 1) | (slot == 4) | (slot == 5)
    is_beta = (slot == 2) | (slot == 3)
    y_g = jnp.where(is_g, g, jnp.where(is_beta, _sigmoid(yg), 0.0))

    mslot = lane & (MLSTM_GATE_SLOTS - 1)
    pre = ym + pm_ref[0:1, :]
    y_m = jnp.where(mslot == 0, pre, jnp.where((mslot == 1) | (mslot == 2), -_softplus(-pre), 0.0))

    ri = lax.broadcasted_iota(jnp.int32, (tg, tg), 0)
    ci = lax.broadcasted_iota(jnp.int32, (tg, tg), 1)
    same = _same_block(ri, ci, CHUNK)
    lower = jnp.where(same & (ri >= ci), 1.0, 0.0)
    total = jnp.where(same, 1.0, 0.0)
    y_all = jnp.concatenate([y_g, y_m], axis=1)
    cum = _mm_exact(lower, y_all)
    tot = _mm_exact(total, y_all)

    g_out = jnp.where(slot < 2, cum[:, :GATE_LANES],
                      jnp.where((slot == 4) | (slot == 5), tot[:, :GATE_LANES], y_g))
    m_out = jnp.where(mslot == 1, cum[:, GATE_LANES:],
                      jnp.where(mslot == 2, tot[:, GATE_LANES:], y_m))
    gcol_ref[...] = g_out
    grow_ref[...] = g_out.T
    mcol_ref[...] = m_out
    mrow_ref[...] = m_out.T


def _gates(x2, mod, nw, wg, wm, pg, pm, batch, seq, tg):
    t, d = x2.shape
    per_seq = seq // tg
    col = jax.ShapeDtypeStruct((t, GATE_LANES), F32)
    row = jax.ShapeDtypeStruct((batch, GATE_LANES, seq), F32)
    col_spec = pl.BlockSpec((tg, GATE_LANES), lambda i: (i, 0))
    row_spec = pl.BlockSpec((None, GATE_LANES, tg), lambda i: (i // per_seq, 0, i % per_seq))
    const = lambda shape: pl.BlockSpec(shape, lambda i: (0, 0))
    return pl.pallas_call(
        functools.partial(_gates_kernel, d=d, tg=tg),
        grid=(t // tg,),
        in_specs=[pl.BlockSpec((tg, d), lambda i: (i, 0)),
                  pl.BlockSpec((None, 1, 3 * d), lambda i: (i // per_seq, 0, 0)),
                  const((1, d)), const((d, GATE_LANES)), const((d, GATE_LANES)),
                  const((2, GATE_LANES)), const((1, GATE_LANES))],
        out_specs=[col_spec, row_spec, col_spec, row_spec],
        out_shape=[col, row, col, row],
        compiler_params=_cparams(1),
    )(x2, mod, nw, wg, wm, pg, pm)


def _unit_lower_inverses(mats, ri, ci):
    diag = _same_block(ri, ci, INV_BASE)
    eye = jnp.where(ri == ci, 1.0, 0.0)
    pows = [jnp.where(diag, a, 0.0) for a in mats]
    invs = [eye - ab for ab in pows]
    order = 2
    while order < INV_BASE:
        pows = [_mm(ab, ab) for ab in pows]
        invs = [inv + _mm(inv, ab) for inv, ab in zip(invs, pows)]
        order *= 2
    size = INV_BASE
    while size < CHUNK:
        merged = _same_block(ri, ci, 2 * size)
        keep = merged & jnp.logical_not(diag)
        corr = [_mm(jnp.where(keep, a, 0.0), inv) for a, inv in zip(mats, invs)]
        invs = [inv - _mm(inv, cr) for inv, cr in zip(invs, corr)]
        diag, size = merged, 2 * size
    return invs


def _lane_slot_shift(group, slots):
    return lax.rem((GATE_LANES // slots - group) * slots, GATE_LANES)


def _group_masks():
    ri = lax.broadcasted_iota(jnp.int32, (GROUP, GROUP), 0)
    ci = lax.broadcasted_iota(jnp.int32, (GROUP, GROUP), 1)
    same = _same_block(ri, ci, CHUNK)
    return ri, ci, same & (ri >= ci), same & (ri > ci)


def _rows(x, g):
    return x[g * GROUP:(g + 1) * GROUP]


def _lanes(x, g):
    return x[:, g * GROUP:(g + 1) * GROUP]


def _gdn_kernel(q_ref, k_ref, v_ref, z_ref, gcol_ref, grow_ref, nw_ref, o_ref, s_ref, *, tt, hps):
    hd = GDN_HEAD_DIM
    n_groups, n_chunks, per_group = tt // GROUP, tt // CHUNK, GROUP // CHUNK

    @pl.when(pl.program_id(2) == 0)
    def _():
        s_ref[...] = jnp.zeros_like(s_ref)

    gcol = pltpu.roll(gcol_ref[...], _lane_slot_shift(pl.program_id(1) * hps, GDN_GATE_SLOTS), axis=1)
    ri, ci, causal, strict = _group_masks()

    heads, a_mats = [], []
    for pp in range(hps):
        q = q_ref[:, pp * hd:(pp + 1) * hd].astype(F32)
        k = k_ref[:, pp * hd:(pp + 1) * hd].astype(F32)
        q = q * lax.rsqrt(jnp.sum(q * q, axis=-1, keepdims=True) + NORM_EPS) * (hd ** -0.5)
        k = k * lax.rsqrt(jnp.sum(k * k, axis=-1, keepdims=True) + NORM_EPS)
        q16, k16, k_t = q.astype(BF16), k.astype(BF16), k.T
        kk = [_mm_nt(_rows(k16, g), _rows(k16, g)) for g in range(n_groups)]
        qk = [_mm_nt(_rows(q16, g), _rows(k16, g)) for g in range(n_groups)]
        for hh in range(2):
            lane = pp * GDN_GATE_SLOTS + hh
            gc_c, be_c = gcol[:, lane:lane + 1], gcol[:, lane + 2:lane + 3]
            gc_r, be_r, gl_r = grow_ref[pp, hh:hh + 1, :], grow_ref[pp, 2 + hh:3 + hh, :], grow_ref[pp, 4 + hh:5 + hh, :]
            decay = [jnp.exp(jnp.where(causal, _rows(gc_c, g) - _lanes(gc_r, g), -jnp.inf)) for g in range(n_groups)]
            a_mats += [jnp.where(strict, _rows(be_c, g) * kk[g] * decay[g], 0.0) for g in range(n_groups)]
            vh = 2 * pp + hh
            heads.append(dict(vh=vh, k16=k16, qk=qk, decay=decay, t_scale=be_r, w_scale=be_r * jnp.exp(gc_r),
                              qd=q * jnp.exp(gc_c), kt_t=(k_t * jnp.exp(gl_r - gc_r)).astype(BF16),
                              g_tot=jnp.exp(gl_r), v=v_ref[:, vh * hd:(vh + 1) * hd]))
    invs = _unit_lower_inverses(a_mats, ri, ci)

    for hi, h in enumerate(heads):
        h["wu"] = []
        for g in range(n_groups):
            inv = invs[hi * n_groups + g]
            w = _mm(inv * _lanes(h["w_scale"], g), _rows(h["k16"], g))
            u = _mm(inv * _lanes(h["t_scale"], g), _rows(h["v"], g))
            h["wu"].append(jnp.concatenate([w, u], axis=1).astype(BF16))
    for h in heads:
        h["awu"] = [_mm(h["qk"][g] * h["decay"][g], h["wu"][g]) for g in range(n_groups)]
    for h in heads:
        h["ktwu"] = []
        for c in range(n_chunks):
            g, r0 = c // per_group, (c % per_group) * CHUNK
            h["ktwu"].append(_mm(h["kt_t"][:, c * CHUNK:(c + 1) * CHUNK], h["wu"][g][r0:r0 + CHUNK]))

    states = [s_ref[h["vh"]] for h in heads]
    outs = [[] for _ in heads]
    for c in range(n_chunks):
        g, r0 = c // per_group, (c % per_group) * CHUNK
        for hi, h in enumerate(heads):
            aw, au = h["awu"][g][r0:r0 + CHUNK, 0:hd], h["awu"][g][r0:r0 + CHUNK, hd:2 * hd]
            q_eff = h["qd"][c * CHUNK:(c + 1) * CHUNK] - aw
            prod = _mm(jnp.concatenate([h["ktwu"][c][:, 0:hd], q_eff], axis=0), states[hi])
            outs[hi].append(prod[hd:hd + CHUNK] + au)
            states[hi] = (states[hi] * h["g_tot"][:, c * CHUNK:c * CHUNK + 1] - prod[0:hd]
                          + h["ktwu"][c][:, hd:2 * hd])
    for hi, h in enumerate(heads):
        vh = h["vh"]
        s_ref[vh] = states[hi]
        o = _rms(jnp.concatenate(outs[hi], axis=0)) * nw_ref[...]
        z = z_ref[:, vh * hd:(vh + 1) * hd].astype(F32)
        o_ref[:, vh * hd:(vh + 1) * hd] = (o * _silu(z)).astype(BF16)


def _gdn(p, gcol, grow, norm_w, batch, seq, tt, hps, col_q, col_k, col_v, col_z):
    t = batch * seq
    hd = GDN_HEAD_DIM
    qw, vw = hps * hd, 2 * hps * hd
    per_seq = seq // tt
    rowi = lambda b, h, i: b * per_seq + i
    return pl.pallas_call(
        functools.partial(_gdn_kernel, tt=tt, hps=hps),
        grid=(batch, GDN_QK_HEADS // hps, per_seq),
        in_specs=[
            pl.BlockSpec((tt, qw), lambda b, h, i: (rowi(b, h, i), col_q // qw + h)),
            pl.BlockSpec((tt, qw), lambda b, h, i: (rowi(b, h, i), col_k // qw + h)),
            pl.BlockSpec((tt, vw), lambda b, h, i: (rowi(b, h, i), col_v // vw + h)),
            pl.BlockSpec((tt, vw), lambda b, h, i: (rowi(b, h, i), col_z // vw + h)),
            pl.BlockSpec((tt, GATE_LANES), lambda b, h, i: (rowi(b, h, i), 0)),
            pl.BlockSpec((None, hps, GDN_GATE_SLOTS, tt), lambda b, h, i: (b, h, 0, i)),
            pl.BlockSpec((1, hd), lambda b, h, i: (0, 0)),
        ],
        out_specs=pl.BlockSpec((tt, vw), lambda b, h, i: (rowi(b, h, i), h)),
        out_shape=jax.ShapeDtypeStruct((t, GDN_V_HEADS * hd), BF16),
        scratch_shapes=[pltpu.VMEM((2 * hps, hd, hd), F32)],
        compiler_params=_cparams(3),
    )(p, p, p, p, gcol, grow, norm_w)


def _mlstm_kernel(q_ref, k_ref, v_ref, og_ref, z_ref, mcol_ref, mrow_ref, nw_ref, o_ref, c_ref, m_ref,
                  *, tt, hps):
    dk, dv = MLSTM_QK_DIM, MLSTM_V_DIM
    n_groups, n_chunks = tt // GROUP, tt // CHUNK

    @pl.when(pl.program_id(2) == 0)
    def _():
        c_ref[...] = jnp.zeros_like(c_ref)
        m_ref[...] = jnp.zeros_like(m_ref)

    mcol = pltpu.roll(mcol_ref[...], _lane_slot_shift(pl.program_id(1) * hps, MLSTM_GATE_SLOTS), axis=1)
    _, _, causal, _ = _group_masks()
    lane = lax.broadcasted_iota(jnp.int32, (tt, 128), 1)
    ones_col = jnp.where(lane == 0, 1.0, 0.0).astype(BF16)
    chunk_shift = CHUNK.bit_length() - 1
    row_chunk = lax.broadcasted_iota(jnp.int32, (tt, 1), 0) >> chunk_shift
    lane_chunk = lax.broadcasted_iota(jnp.int32, (1, tt), 1) >> chunk_shift

    heads = []
    for hh in range(hps):
        q16, k16 = q_ref[:, hh * dk:(hh + 1) * dk], k_ref[:, hh * dk:(hh + 1) * dk]
        v_aug = jnp.concatenate([v_ref[:, hh * dv:(hh + 1) * dv], ones_col], axis=1)
        lane0 = hh * MLSTM_GATE_SLOTS
        b_c = mcol[:, lane0 + 1:lane0 + 2]
        i_r, b_r, bl_r = mrow_ref[hh, 0:1, :], mrow_ref[hh, 1:2, :], mrow_ref[hh, 2:3, :]
        log_end = bl_r - b_r + i_r
        m = m_ref[hh, 0:1, 0:1]
        m_start_c = jnp.zeros((tt, 1), F32)
        m_start_r = jnp.zeros((1, tt), F32)
        m_next_r = jnp.zeros((1, tt), F32)
        for c in range(n_chunks):
            r0, r1 = c * CHUNK, (c + 1) * CHUNK
            m_next = jnp.maximum(bl_r[:, r0:r0 + 1] + m, jnp.max(log_end[:, r0:r1], axis=1, keepdims=True))
            m_start_c = jnp.where(row_chunk == c, m, m_start_c)
            m_start_r = jnp.where(lane_chunk == c, m, m_start_r)
            m_next_r = jnp.where(lane_chunk == c, m_next, m_next_r)
            m = m_next
        m_ref[hh] = jnp.broadcast_to(m, m_ref.shape[1:])
        k_ts = (k16.astype(F32).T * jnp.exp(log_end - m_next_r)).astype(BF16)
        heads.append(dict(q16=q16, k16=k16, v_aug=v_aug, b_c=b_c, b_r=b_r, i_r=i_r, m_inter=b_c + m_start_c,
                          k_ts=k_ts, carry=jnp.exp(bl_r + m_start_r - m_next_r)))

    for h in heads:
        h["qk"] = [_mm_nt(_rows(h["q16"], g), _rows(h["k16"], g)) for g in range(n_groups)]
    for h in heads:
        m_ts, intras = [], []
        for g in range(n_groups):
            log_d = jnp.where(causal, _rows(h["b_c"], g) - _lanes(h["b_r"], g) + _lanes(h["i_r"], g), -jnp.inf)
            m_t = jnp.maximum(_rows(h["m_inter"], g), jnp.max(log_d, axis=1, keepdims=True))
            intras.append(_mm(h["qk"][g] * jnp.exp(log_d - m_t), _rows(h["v_aug"], g)))
            m_ts.append(m_t)
        h["m_t"], h["intra"] = jnp.concatenate(m_ts, axis=0), jnp.concatenate(intras, axis=0)
    for h in heads:
        h["kv"] = [_mm(h["k_ts"][:, c * CHUNK:(c + 1) * CHUNK], h["v_aug"][c * CHUNK:(c + 1) * CHUNK])
                   for c in range(n_chunks)]

    for hh, h in enumerate(heads):
        state = c_ref[hh]
        inters = []
        for c in range(n_chunks):
            inters.append(_mm(h["q16"][c * CHUNK:(c + 1) * CHUNK], state))
            state = state * h["carry"][:, c * CHUNK:c * CHUNK + 1] + h["kv"][c]
        c_ref[hh] = state
        m_t = h["m_t"]
        num = jnp.exp(h["m_inter"] - m_t) * jnp.concatenate(inters, axis=0) + h["intra"]
        den = jnp.maximum(jnp.abs(num[:, dv:dv + 1]), jnp.exp(-m_t))
        hn = _rms(num[:, 0:dv] / den) * nw_ref[:, hh * dv:(hh + 1) * dv]
        cols = slice(hh * dv, (hh + 1) * dv)
        o_ref[:, cols] = (_sigmoid(og_ref[:, cols].astype(F32)) * hn * _silu(z_ref[:, cols].astype(F32))).astype(BF16)


def _mlstm(p, mcol, mrow, norm_w, batch, seq, tt, hps, col_q, col_k, col_v, col_o, col_z):
    t = batch * seq
    dk, dv = MLSTM_QK_DIM, MLSTM_V_DIM
    qw, vw = hps * dk, hps * dv
    per_seq = seq // tt
    rowi = lambda b, h, i: b * per_seq + i
    return pl.pallas_call(
        functools.partial(_mlstm_kernel, tt=tt, hps=hps),
        grid=(batch, MLSTM_HEADS // hps, per_seq),
        in_specs=[
            pl.BlockSpec((tt, qw), lambda b, h, i: (rowi(b, h, i), col_q // qw + h)),
            pl.BlockSpec((tt, qw), lambda b, h, i: (rowi(b, h, i), col_k // qw + h)),
            pl.BlockSpec((tt, vw), lambda b, h, i: (rowi(b, h, i), col_v // vw + h)),
            pl.BlockSpec((tt, vw), lambda b, h, i: (rowi(b, h, i), col_o // vw + h)),
            pl.BlockSpec((tt, vw), lambda b, h, i: (rowi(b, h, i), col_z // vw + h)),
            pl.BlockSpec((tt, GATE_LANES), lambda b, h, i: (rowi(b, h, i), 0)),
            pl.BlockSpec((None, hps, MLSTM_GATE_SLOTS, tt), lambda b, h, i: (b, h, 0, i)),
            pl.BlockSpec((1, vw), lambda b, h, i: (0, h)),
        ],
        out_specs=pl.BlockSpec((tt, vw), lambda b, h, i: (rowi(b, h, i), h)),
        out_shape=jax.ShapeDtypeStruct((t, MLSTM_HEADS * dv), BF16),
        scratch_shapes=[pltpu.VMEM((hps, dk, dv + 128), F32), pltpu.VMEM((hps, 8, 128), F32)],
        compiler_params=_cparams(3),
    )(p, p, p, p, p, mcol, mrow, norm_w)


def _merge_kernel(ya_ref, hb_ref, wa_ref, wb_ref, ga_ref, gb_ref, o_ref):
    ya = jnp.dot(ya_ref[...], wa_ref[...], preferred_element_type=F32)
    yb = jnp.dot(hb_ref[...], wb_ref[...], preferred_element_type=F32)
    merged = _sigmoid(ga_ref[...].astype(F32)) * ya + _sigmoid(gb_ref[...].astype(F32)) * yb
    o_ref[...] = merged.astype(BF16)


def _merge(ya, hb, wa, wb, p, col_ga, col_gb, tm, tn):
    t = ya.shape[0]
    d = wa.shape[1]
    return pl.pallas_call(
        _merge_kernel,
        grid=(t // tm, d // tn),
        in_specs=[pl.BlockSpec((tm, ya.shape[1]), lambda i, j: (i, 0)),
                  pl.BlockSpec((tm, hb.shape[1]), lambda i, j: (i, 0)),
                  pl.BlockSpec((wa.shape[0], tn), lambda i, j: (0, j)),
                  pl.BlockSpec((wb.shape[0], tn), lambda i, j: (0, j)),
                  pl.BlockSpec((tm, tn), lambda i, j: (i, col_ga // tn + j)),
                  pl.BlockSpec((tm, tn), lambda i, j: (i, col_gb // tn + j))],
        out_specs=pl.BlockSpec((tm, tn), lambda i, j: (i, j)),
        out_shape=jax.ShapeDtypeStruct((t, d), BF16),
        compiler_params=_cparams(2),
    )(ya, hb, wa, wb, p, p)


def _out_kernel(mg_ref, w_ref, x_ref, mod_ref, nw_ref, o_ref, *, d):
    out = jnp.dot(mg_ref[...], w_ref[...], preferred_element_type=F32)
    o_ref[...] = x_ref[...] + mod_ref[:, 2 * d:3 * d] * (_rms(out) * nw_ref[...])


def _out(mg, w_out, x2, mod, nw, seq, tm):
    t, d = x2.shape
    per_seq = seq // tm
    return pl.pallas_call(
        functools.partial(_out_kernel, d=d),
        grid=(t // tm,),
        in_specs=[pl.BlockSpec((tm, d), lambda i: (i, 0)),
                  pl.BlockSpec((d, d), lambda i: (0, 0)),
                  pl.BlockSpec((tm, d), lambda i: (i, 0)),
                  pl.BlockSpec((None, 1, 3 * d), lambda i: (i // per_seq, 0, 0)),
                  pl.BlockSpec((1, d), lambda i: (0, 0))],
        out_specs=pl.BlockSpec((tm, d), lambda i: (i, 0)),
        out_shape=jax.ShapeDtypeStruct((t, d), F32),
        compiler_params=_cparams(1),
    )(mg, w_out, x2, mod, nw)


def _gate_columns(d):
    qkv = 2 * GDN_QK_HEADS * GDN_HEAD_DIM + GDN_V_HEADS * GDN_HEAD_DIM
    col_a, col_b = qkv, qkv + GDN_V_HEADS
    col_i = col_b + GDN_V_HEADS + GDN_V_HEADS * GDN_HEAD_DIM + 2 * MLSTM_HEADS * MLSTM_QK_DIM + MLSTM_HEADS * MLSTM_V_DIM
    col_f = col_i + MLSTM_HEADS
    g_src, g_head, g_use = np.zeros(GATE_LANES, np.int32), np.zeros(GATE_LANES, np.int32), np.zeros(GATE_LANES, bool)
    m_src, m_head, m_use = np.zeros(GATE_LANES, np.int32), np.zeros(GATE_LANES, np.int32), np.zeros(GATE_LANES, bool)
    for lane in range(GATE_LANES):
        pair, slot = divmod(lane, GDN_GATE_SLOTS)
        if slot < 6:
            head = 2 * pair + slot % 2
            g_src[lane] = (col_b if slot in (2, 3) else col_a) + head
            g_head[lane], g_use[lane] = head, True
        head, slot = divmod(lane, MLSTM_GATE_SLOTS)
        if slot < 3:
            m_src[lane] = (col_i if slot == 0 else col_f) + head
            m_head[lane], m_use[lane] = head, True
    slot_is_i = (np.arange(GATE_LANES) % MLSTM_GATE_SLOTS) == 0
    return (g_src, g_head, g_use), (m_src, m_head, m_use), slot_is_i


def _layer(x, c, w_ada, b_ada, norm_pre_w, w_in, gdn_conv_w, gdn_A_log, gdn_dt_bias, gdn_norm_w,
           mlstm_conv_w, mlstm_b_i, mlstm_b_f, mlstm_norm_w, w_proj_gdn, w_proj_mlstm, w_out, norm_post_w):
    batch, seq, d = x.shape
    t = batch * seq
    qk_w = GDN_QK_HEADS * GDN_HEAD_DIM
    v_w = GDN_V_HEADS * GDN_HEAD_DIM
    mqk_w = MLSTM_HEADS * MLSTM_QK_DIM
    mv_w = MLSTM_HEADS * MLSTM_V_DIM
    assert d == qk_w == mv_w and seq % 256 == 0

    src_a = 2 * qk_w + v_w
    src_z = src_a + 2 * GDN_V_HEADS
    src_mqk = src_z + v_w
    src_i = src_mqk + 2 * mqk_w + mv_w
    src_o = src_i + 2 * MLSTM_HEADS
    col_q, col_k, col_v, col_z = 0, qk_w, 2 * qk_w, 2 * qk_w + v_w
    col_mq = col_z + v_w
    col_mk, col_mv = col_mq + mqk_w, col_mq + 2 * mqk_w
    col_mo = col_mv + mv_w
    col_mz, col_ga, col_gb = col_mo + mv_w, col_mo + 2 * mv_w, col_mo + 2 * mv_w + d
    n_p = col_gb + d
    w_big = jnp.concatenate([w_in[:, :src_a], w_in[:, src_z:src_i], w_in[:, src_o:]], axis=1).astype(BF16)
    assert w_big.shape[1] == n_p

    tn = 1024
    conv_big = jnp.zeros((8, n_p), F32).at[CONV_WIDTH, :].set(1.0)
    conv_big = conv_big.at[0:CONV_WIDTH, col_q:col_z].set(gdn_conv_w)
    conv_big = conv_big.at[0:CONV_WIDTH, col_mq:col_mv].set(mlstm_conv_w)
    conv_big = conv_big.at[CONV_WIDTH, col_mq:col_mk].set(MLSTM_QK_DIM ** -0.5)
    assert col_z % tn == 0 and col_mq % tn == 0 and col_mv % tn == 0
    conv_tiles = ((col_q // tn, col_z // tn), (col_mq // tn, col_mv // tn))

    (g_src, g_head, g_use), (m_src, m_head, m_use), slot_is_i = _gate_columns(d)
    wg = jnp.where(g_use[None, :], jnp.take(w_in, g_src, axis=1), 0.0)
    wm = jnp.where(m_use[None, :], jnp.take(w_in, m_src, axis=1), 0.0)
    pg = jnp.stack([jnp.where(g_use, jnp.take(gdn_A_log, g_head), 0.0),
                    jnp.where(g_use, jnp.take(gdn_dt_bias, g_head), 0.0)])
    pm = jnp.where(m_use, jnp.where(slot_is_i, jnp.take(mlstm_b_i, m_head), jnp.take(mlstm_b_f, m_head)), 0.0)[None, :]

    x2 = x.reshape(t, d)
    rows = max(8, batch)
    c8 = jnp.pad(c, ((0, rows - batch), (0, 0)))
    mod = _ada(c8, w_ada, b_ada)[:batch].reshape(batch, 1, 3 * d)
    nw_pre = norm_pre_w.reshape(1, d)

    tm = min(1024, seq)
    p = _inproj(x2, mod, nw_pre, w_big, conv_big, seq, tm=tm, tn=tn, conv_tiles=conv_tiles)
    gcol, grow, mcol, mrow = _gates(x2, mod, nw_pre, wg, wm, pg, pm, batch, seq, tg=min(512, seq))
    grow = grow.reshape(batch, GDN_QK_HEADS, GDN_GATE_SLOTS, seq)
    mrow = mrow.reshape(batch, MLSTM_HEADS, MLSTM_GATE_SLOTS, seq)

    tt = 256
    ya = _gdn(p, gcol, grow, gdn_norm_w.reshape(1, GDN_HEAD_DIM), batch, seq, tt, 4,
              col_q, col_k, col_v, col_z)
    hb = _mlstm(p, mcol, mrow, mlstm_norm_w.reshape(1, mv_w), batch, seq, tt, 2,
                col_mq, col_mk, col_mv, col_mo, col_mz)
    mg = _merge(ya, hb, w_proj_gdn.astype(BF16), w_proj_mlstm.astype(BF16), p, col_ga, col_gb, tm=tm, tn=512)
    y = _out(mg, w_out.astype(BF16), x2, mod, norm_post_w.reshape(1, d), seq, tm=min(512, seq))
    return y.reshape(batch, seq, d)


def kernel(x, c, w_ada, b_ada, norm_pre_w, w_in, gdn_conv_w, gdn_A_log, gdn_dt_bias, gdn_norm_w, mlstm_conv_w, mlstm_b_i, mlstm_b_f, mlstm_norm_w, w_proj_gdn, w_proj_mlstm, w_out, norm_post_w):
    for l in range(w_ada.shape[0]):
        x = _layer(x, c, w_ada[l], b_ada[l], norm_pre_w[l], w_in[l], gdn_conv_w[l], gdn_A_log[l],
                   gdn_dt_bias[l], gdn_norm_w[l], mlstm_conv_w[l], mlstm_b_i[l], mlstm_b_f[l],
                   mlstm_norm_w[l], w_proj_gdn[l], w_proj_mlstm[l], w_out[l], norm_post_w[l])
    return x
```

```python
import functools

import jax
import jax.numpy as jnp
import numpy as np
from jax import lax
from jax.experimental import pallas as pl
from jax.experimental.pallas import tpu as pltpu

F32 = jnp.float32
BF16 = jnp.bfloat16

NORM_EPS = 1e-6
CHUNK = 64
CONV_WIDTH = 4
CONV_HALO = 8
INV_BASE = 16
GROUP = 128

GDN_QK_HEADS = 16
GDN_V_HEADS = 32
GDN_HEAD_DIM = 128
MLSTM_HEADS = 8
MLSTM_QK_DIM = 128
MLSTM_V_DIM = 256
GATE_LANES = 128
GDN_GATE_SLOTS = GATE_LANES // GDN_QK_HEADS
MLSTM_GATE_SLOTS = GATE_LANES // MLSTM_HEADS

VMEM_LIMIT_BYTES = 56 * 1024 * 1024


def _cparams(n_axes):
    return pltpu.CompilerParams(dimension_semantics=("arbitrary",) * n_axes,
                                vmem_limit_bytes=VMEM_LIMIT_BYTES)


def _mm(a, b):
    return jnp.dot(a.astype(BF16), b.astype(BF16), preferred_element_type=F32)


def _mm_nt(a, b):
    return lax.dot_general(a.astype(BF16), b.astype(BF16), (((1,), (1,)), ((), ())),
                           preferred_element_type=F32)


def _sigmoid(x):
    return 1.0 / (1.0 + jnp.exp(-x))


def _silu(x):
    return x * _sigmoid(x)


def _softplus(x):
    return jnp.maximum(x, 0.0) + jnp.log(1.0 + jnp.exp(-jnp.abs(x)))


def _rms(x):
    return x * lax.rsqrt(jnp.mean(x * x, axis=-1, keepdims=True) + NORM_EPS)


def _prenorm(x_ref, mod_ref, nw_ref, d):
    y = _rms(x_ref[...]) * nw_ref[...]
    return y * (1.0 + mod_ref[:, d:2 * d]) + mod_ref[:, 0:d]


def _ada_kernel(c_ref, w_ref, b_ref, o_ref):
    o_ref[...] = _mm(_silu(c_ref[...]), w_ref[...]) + b_ref[...]


def _ada(c8, w_ada, b_ada, tn=512):
    rows, d = c8.shape
    n = w_ada.shape[1]
    return pl.pallas_call(
        _ada_kernel,
        grid=(n // tn,),
        in_specs=[pl.BlockSpec((rows, d), lambda j: (0, 0)),
                  pl.BlockSpec((d, tn), lambda j: (0, j)),
                  pl.BlockSpec((1, tn), lambda j: (0, j))],
        out_specs=pl.BlockSpec((rows, tn), lambda j: (0, j)),
        out_shape=jax.ShapeDtypeStruct((rows, n), F32),
        compiler_params=_cparams(1),
    )(c8, w_ada, b_ada.reshape(1, n))


def _inproj_kernel(x_ref, mod_ref, nw_ref, w_ref, cw_ref, o_ref, h_ref, tail_ref, *acc_refs,
                   d, tm, sub, per_seq, conv_tiles):
    i, j = pl.program_id(0), pl.program_id(1)

    @pl.when(j == 0)
    def _():
        h_ref[...] = _prenorm(x_ref, mod_ref, nw_ref, d).astype(BF16)

    is_conv = functools.reduce(jnp.logical_or, [(j >= lo) & (j < hi) for lo, hi in conv_tiles])

    @pl.when(jnp.logical_not(is_conv))
    def _():
        for r in range(tm // sub):
            o_ref[r * sub:(r + 1) * sub, :] = jnp.dot(
                h_ref[r * sub:(r + 1) * sub, :], w_ref[...], preferred_element_type=F32).astype(BF16)

    @pl.when(is_conv)
    def _():
        first = (i % per_seq) == 0

        @pl.when(first)
        def _():
            acc_refs[0][0:CONV_HALO, :] = jnp.zeros((CONV_HALO, w_ref.shape[1]), F32)

        @pl.when(jnp.logical_not(first))
        def _():
            acc_refs[0][0:CONV_HALO, :] = tail_ref[j]

        cw = cw_ref[...]
        for r, acc_ref in enumerate(acc_refs):
            acc = jnp.dot(h_ref[r * sub:(r + 1) * sub, :], w_ref[...], preferred_element_type=F32)
            acc_ref[CONV_HALO:CONV_HALO + sub, :] = acc
            if r + 1 < len(acc_refs):
                acc_refs[r + 1][0:CONV_HALO, :] = acc[sub - CONV_HALO:sub, :]
            else:
                tail_ref[j] = acc[sub - CONV_HALO:sub, :]
            y = cw[CONV_WIDTH - 1:CONV_WIDTH, :] * acc
            for tap in range(CONV_WIDTH - 1):
                back = CONV_WIDTH - 1 - tap
                y = y + cw[tap:tap + 1, :] * acc_ref[CONV_HALO - back:CONV_HALO - back + sub, :]
            o_ref[r * sub:(r + 1) * sub, :] = (_silu(y) * cw[CONV_WIDTH:CONV_WIDTH + 1, :]).astype(BF16)


def _inproj(x2, mod, nw, w_big, conv_big, seq, tm, tn, conv_tiles):
    t, d = x2.shape
    n = w_big.shape[1]
    per_seq = seq // tm
    sub = min(256, tm)
    return pl.pallas_call(
        functools.partial(_inproj_kernel, d=d, tm=tm, sub=sub, per_seq=per_seq, conv_tiles=conv_tiles),
        grid=(t // tm, n // tn),
        in_specs=[pl.BlockSpec((tm, d), lambda i, j: (i, 0)),
                  pl.BlockSpec((None, 1, 3 * d), lambda i, j: (i // per_seq, 0, 0)),
                  pl.BlockSpec((1, d), lambda i, j: (0, 0)),
                  pl.BlockSpec((d, tn), lambda i, j: (0, j)),
                  pl.BlockSpec((8, tn), lambda i, j: (0, j))],
        out_specs=pl.BlockSpec((tm, tn), lambda i, j: (i, j)),
        out_shape=jax.ShapeDtypeStruct((t, n), BF16),
        scratch_shapes=[pltpu.VMEM((tm, d), BF16), pltpu.VMEM((n // tn, CONV_HALO, tn), F32)]
        + [pltpu.VMEM((CONV_HALO + sub, tn), F32)] * (tm // sub),
        compiler_params=_cparams(2),
    )(x2, mod, nw, w_big, conv_big)


def _same_block(ri, ci, size):
    shift = size.bit_length() - 1
    return (ri >> shift) == (ci >> shift)


def _gates_kernel(x_ref, mod_ref, nw_ref, whi_ref, wlo_ref, pg_ref, pm_ref,
                  gcol_ref, grow_ref, mcol_ref, mrow_ref, mstate_ref, *, d, tg, per_seq):
    h = _prenorm(x_ref, mod_ref, nw_ref, d)
    h_hi = h.astype(BF16)
    h_lo = (h - h_hi.astype(F32)).astype(BF16)
    w_hi, w_lo = whi_ref[...], wlo_ref[...]
    y = (jnp.dot(h_hi, w_hi, preferred_element_type=F32)
         + (jnp.dot(h_hi, w_lo, preferred_element_type=F32) + jnp.dot(h_lo, w_hi, preferred_element_type=F32)))
    yg, ym = y[:, :GATE_LANES], y[:, GATE_LANES:]
    lane = lax.broadcasted_iota(jnp.int32, (tg, GATE_LANES), 1)

    slot = lane & (GDN_GATE_SLOTS - 1)
    g = -jnp.exp(pg_ref[0:1, :]) * _softplus(yg + pg_ref[1:2, :])
    is_g = (slot == 0) | (slot == 1) | (slot == 4) | (slot == 5)
    is_beta = (slot == 2) | (slot == 3)
    y_g = jnp.where(is_g, g, jnp.where(is_beta, _sigmoid(yg), 0.0))

    mslot = lane & (MLSTM_GATE_SLOTS - 1)
    pre = ym + pm_ref[0:1, :]
    y_m = jnp.where(mslot == 0, pre, jnp.where((mslot == 1) | (mslot == 2), -_softplus(-pre), 0.0))

    ri = lax.broadcasted_iota(jnp.int32, (tg, tg), 0)
    ci = lax.broadcasted_iota(jnp.int32, (tg, tg), 1)
    same = _same_block(ri, ci, CHUNK)
    lower = jnp.where(same & (ri >= ci), 1.0, 0.0).astype(BF16)
    total = jnp.where(same, 1.0, 0.0).astype(BF16)
    y_all = jnp.concatenate([y_g, y_m], axis=1)
    parts = []
    rest = y_all
    for _ in range(3):
        piece = rest.astype(BF16)
        parts.append(piece)
        rest = rest - piece.astype(F32)
    cum = sum(jnp.dot(lower, piece, preferred_element_type=F32) for piece in reversed(parts))
    tot = sum(jnp.dot(total, piece, preferred_element_type=F32) for piece in reversed(parts))

    g_out = jnp.where(slot < 2, cum[:, :GATE_LANES],
                      jnp.where((slot == 4) | (slot == 5), tot[:, :GATE_LANES], y_g))
    m_out = jnp.where(mslot == 1, cum[:, GATE_LANES:],
                      jnp.where(mslot == 2, tot[:, GATE_LANES:], y_m))

    @pl.when(pl.program_id(0) % per_seq == 0)
    def _():
        mstate_ref[...] = jnp.zeros_like(mstate_ref)

    b_al = pltpu.roll(m_out, GATE_LANES - 1, axis=1)
    bl_al = pltpu.roll(m_out, GATE_LANES - 2, axis=1)
    log_end = bl_al - b_al + m_out
    m = mstate_ref[0:1, :]
    m_starts, m_nexts = [], []
    for c in range(tg // CHUNK):
        r0 = c * CHUNK
        m_starts.append(jnp.broadcast_to(m, (CHUNK, GATE_LANES)))
        m = jnp.maximum(bl_al[r0:r0 + 1, :] + m, jnp.max(log_end[r0:r0 + CHUNK, :], axis=0, keepdims=True))
        m_nexts.append(jnp.broadcast_to(m, (CHUNK, GATE_LANES)))
    mstate_ref[...] = jnp.broadcast_to(m, mstate_ref.shape)
    m_out = jnp.where(mslot == 3, pltpu.roll(jnp.concatenate(m_starts, axis=0), 3, axis=1),
                      jnp.where(mslot == 4, pltpu.roll(jnp.concatenate(m_nexts, axis=0), 4, axis=1), m_out))
    gcol_ref[...] = g_out
    grow_ref[...] = g_out.T
    mcol_ref[...] = m_out
    mrow_ref[...] = m_out.T


def _gates(x2, mod, nw, w_gates, pg, pm, batch, seq, tg):
    t, d = x2.shape
    per_seq = seq // tg
    w_hi = w_gates.astype(BF16)
    w_lo = (w_gates - w_hi.astype(F32)).astype(BF16)
    col = jax.ShapeDtypeStruct((t, GATE_LANES), F32)
    row = jax.ShapeDtypeStruct((batch, GATE_LANES, seq), F32)
    col_spec = pl.BlockSpec((tg, GATE_LANES), lambda i: (i, 0))
    row_spec = pl.BlockSpec((None, GATE_LANES, tg), lambda i: (i // per_seq, 0, i % per_seq))
    const = lambda shape: pl.BlockSpec(shape, lambda i: (0, 0))
    return pl.pallas_call(
        functools.partial(_gates_kernel, d=d, tg=tg, per_seq=per_seq),
        grid=(t // tg,),
        in_specs=[pl.BlockSpec((tg, d), lambda i: (i, 0)),
                  pl.BlockSpec((None, 1, 3 * d), lambda i: (i // per_seq, 0, 0)),
                  const((1, d)), const((d, 2 * GATE_LANES)), const((d, 2 * GATE_LANES)),
                  const((2, GATE_LANES)), const((1, GATE_LANES))],
        out_specs=[col_spec, row_spec, col_spec, row_spec],
        out_shape=[col, row, col, row],
        scratch_shapes=[pltpu.VMEM((8, GATE_LANES), F32)],
        compiler_params=_cparams(1),
    )(x2, mod, nw, w_hi, w_lo, pg, pm)


def _unit_lower_inverses(mats, ri, ci):
    diag = _same_block(ri, ci, INV_BASE)
    eye = jnp.where(ri == ci, 1.0, 0.0)
    pows = [jnp.where(diag, a, 0.0) for a in mats]
    invs = [eye - ab for ab in pows]
    order = 2
    while order < INV_BASE:
        pows = [_mm(ab, ab) for ab in pows]
        invs = [inv + _mm(inv, ab) for inv, ab in zip(invs, pows)]
        order *= 2
    size = INV_BASE
    while size < CHUNK:
        merged = _same_block(ri, ci, 2 * size)
        keep = merged & jnp.logical_not(diag)
        corr = [_mm(jnp.where(keep, a, 0.0), inv) for a, inv in zip(mats, invs)]
        invs = [inv - _mm(inv, cr) for inv, cr in zip(invs, corr)]
        diag, size = merged, 2 * size
    return invs


def _lane_slot_shift(group, slots):
    return lax.rem((GATE_LANES // slots - group) * slots, GATE_LANES)


def _group_masks():
    ri = lax.broadcasted_iota(jnp.int32, (GROUP, GROUP), 0)
    ci = lax.broadcasted_iota(jnp.int32, (GROUP, GROUP), 1)
    same = _same_block(ri, ci, CHUNK)
    return ri, ci, same & (ri >= ci), same & (ri > ci)


def _rows(x, g):
    return x[g * GROUP:(g + 1) * GROUP]


def _lanes(x, g):
    return x[:, g * GROUP:(g + 1) * GROUP]


def _gdn_kernel(q_ref, k_ref, v_ref, z_ref, gcol_ref, grow_ref, nw_ref, o_ref, s_ref, *, tt, hps):
    hd = GDN_HEAD_DIM
    n_groups, n_chunks, per_group = tt // GROUP, tt // CHUNK, GROUP // CHUNK

    @pl.when(pl.program_id(2) == 0)
    def _():
        s_ref[...] = jnp.zeros_like(s_ref)

    gcol = pltpu.roll(gcol_ref[...], _lane_slot_shift(pl.program_id(1) * hps, GDN_GATE_SLOTS), axis=1)
    ri, ci, causal, strict = _group_masks()

    heads, a_mats = [], []
    for pp in range(hps):
        q = q_ref[:, pp * hd:(pp + 1) * hd].astype(F32)
        k = k_ref[:, pp * hd:(pp + 1) * hd].astype(F32)
        q = q * lax.rsqrt(jnp.sum(q * q, axis=-1, keepdims=True) + NORM_EPS) * (hd ** -0.5)
        k = k * lax.rsqrt(jnp.sum(k * k, axis=-1, keepdims=True) + NORM_EPS)
        q16, k16, k_t = q.astype(BF16), k.astype(BF16), k.T
        kk = [_mm_nt(_rows(k16, g), _rows(k16, g)) for g in range(n_groups)]
        qk = [_mm_nt(_rows(q16, g), _rows(k16, g)) for g in range(n_groups)]
        for hh in range(2):
            lane = pp * GDN_GATE_SLOTS + hh
            gc_c, be_c = gcol[:, lane:lane + 1], gcol[:, lane + 2:lane + 3]
            gc_r, be_r, gl_r = grow_ref[pp, hh:hh + 1, :], grow_ref[pp, 2 + hh:3 + hh, :], grow_ref[pp, 4 + hh:5 + hh, :]
            decay = [jnp.exp(jnp.where(causal, _rows(gc_c, g) - _lanes(gc_r, g), -jnp.inf)) for g in range(n_groups)]
            a_mats += [jnp.where(strict, _rows(be_c, g) * kk[g] * decay[g], 0.0) for g in range(n_groups)]
            vh = 2 * pp + hh
            heads.append(dict(vh=vh, k16=k16, qk=qk, decay=decay, t_scale=be_r, w_scale=be_r * jnp.exp(gc_r),
                              qd=q * jnp.exp(gc_c), kt_t=(k_t * jnp.exp(gl_r - gc_r)).astype(BF16),
                              g_tot=jnp.exp(gl_r), v=v_ref[:, vh * hd:(vh + 1) * hd]))
    invs = _unit_lower_inverses(a_mats, ri, ci)

    for hi, h in enumerate(heads):
        h["wu"] = []
        for g in range(n_groups):
            inv = invs[hi * n_groups + g]
            w = _mm(inv * _lanes(h["w_scale"], g), _rows(h["k16"], g))
            u = _mm(inv * _lanes(h["t_scale"], g), _rows(h["v"], g))
            h["wu"].append(jnp.concatenate([w, u], axis=1).astype(BF16))
    for h in heads:
        h["awu"] = [_mm(h["qk"][g] * h["decay"][g], h["wu"][g]) for g in range(n_groups)]
    for h in heads:
        h["ktwu"] = []
        for c in range(n_chunks):
            g, r0 = c // per_group, (c % per_group) * CHUNK
            h["ktwu"].append(_mm(h["kt_t"][:, c * CHUNK:(c + 1) * CHUNK], h["wu"][g][r0:r0 + CHUNK]))

    states = [s_ref[h["vh"]] for h in heads]
    outs = [[] for _ in heads]
    for c in range(n_chunks):
        g, r0 = c // per_group, (c % per_group) * CHUNK
        for hi, h in enumerate(heads):
            aw, au = h["awu"][g][r0:r0 + CHUNK, 0:hd], h["awu"][g][r0:r0 + CHUNK, hd:2 * hd]
            q_eff = h["qd"][c * CHUNK:(c + 1) * CHUNK] - aw
            prod = _mm(jnp.concatenate([h["ktwu"][c][:, 0:hd], q_eff], axis=0), states[hi])
            outs[hi].append(prod[hd:hd + CHUNK] + au)
            states[hi] = (states[hi] * h["g_tot"][:, c * CHUNK:c * CHUNK + 1] - prod[0:hd]
                          + h["ktwu"][c][:, hd:2 * hd])
    for hi, h in enumerate(heads):
        vh = h["vh"]
        s_ref[vh] = states[hi]
        o = _rms(jnp.concatenate(outs[hi], axis=0)) * nw_ref[...]
        z = z_ref[:, vh * hd:(vh + 1) * hd].astype(F32)
        o_ref[:, vh * hd:(vh + 1) * hd] = (o * _silu(z)).astype(BF16)


def _gdn(p, gcol, grow, norm_w, batch, seq, tt, hps, col_q, col_k, col_v, col_z):
    t = batch * seq
    hd = GDN_HEAD_DIM
    qw, vw = hps * hd, 2 * hps * hd
    per_seq = seq // tt
    rowi = lambda b, h, i: b * per_seq + i
    return pl.pallas_call(
        functools.partial(_gdn_kernel, tt=tt, hps=hps),
        grid=(batch, GDN_QK_HEADS // hps, per_seq),
        in_specs=[
            pl.BlockSpec((tt, qw), lambda b, h, i: (rowi(b, h, i), col_q // qw + h)),
            pl.BlockSpec((tt, qw), lambda b, h, i: (rowi(b, h, i), col_k // qw + h)),
            pl.BlockSpec((tt, vw), lambda b, h, i: (rowi(b, h, i), col_v // vw + h)),
            pl.BlockSpec((tt, vw), lambda b, h, i: (rowi(b, h, i), col_z // vw + h)),
            pl.BlockSpec((tt, GATE_LANES), lambda b, h, i: (rowi(b, h, i), 0)),
            pl.BlockSpec((None, hps, GDN_GATE_SLOTS, tt), lambda b, h, i: (b, h, 0, i)),
            pl.BlockSpec((1, hd), lambda b, h, i: (0, 0)),
        ],
        out_specs=pl.BlockSpec((tt, vw), lambda b, h, i: (rowi(b, h, i), h)),
        out_shape=jax.ShapeDtypeStruct((t, GDN_V_HEADS * hd), BF16),
        scratch_shapes=[pltpu.VMEM((2 * hps, hd, hd), F32)],
        compiler_params=_cparams(3),
    )(p, p, p, p, gcol, grow, norm_w)


def _mlstm_kernel(q_ref, k_ref, v_ref, og_ref, z_ref, mcol_ref, mrow_ref, nw_ref, o_ref, c_ref, *, tt, hps):
    dk, dv = MLSTM_QK_DIM, MLSTM_V_DIM
    n_groups, n_chunks = tt // GROUP, tt // CHUNK

    @pl.when(pl.program_id(2) == 0)
    def _():
        c_ref[...] = jnp.zeros_like(c_ref)

    mcol = pltpu.roll(mcol_ref[...], _lane_slot_shift(pl.program_id(1) * hps, MLSTM_GATE_SLOTS), axis=1)
    _, _, causal, _ = _group_masks()
    lane = lax.broadcasted_iota(jnp.int32, (tt, 128), 1)
    ones_col = jnp.where(lane == 0, 1.0, 0.0).astype(BF16)

    heads = []
    for hh in range(hps):
        q16, k16 = q_ref[:, hh * dk:(hh + 1) * dk], k_ref[:, hh * dk:(hh + 1) * dk]
        v_aug = jnp.concatenate([v_ref[:, hh * dv:(hh + 1) * dv], ones_col], axis=1)
        lane0 = hh * MLSTM_GATE_SLOTS
        b_c, m_start_c = mcol[:, lane0 + 1:lane0 + 2], mcol[:, lane0 + 3:lane0 + 4]
        i_r, b_r, bl_r = mrow_ref[hh, 0:1, :], mrow_ref[hh, 1:2, :], mrow_ref[hh, 2:3, :]
        m_start_r, m_next_r = mrow_ref[hh, 3:4, :], mrow_ref[hh, 4:5, :]
        log_end = bl_r - b_r + i_r
        k_ts =(k16.astype(F32).T * jnp.exp(log_end - m_next_r)).astype(BF16)
        heads.append(dict(q16=q16, k16=k16, v_aug=v_aug, b_c=b_c, b_r=b_r, i_r=i_r, m_inter=b_c + m_start_c,
                          k_ts=k_ts, carry=jnp.exp(bl_r + m_start_r - m_next_r)))

    probs = [(h, g) for h in heads for g in range(n_groups)]
    qks = [_mm_nt(_rows(h["q16"], g), _rows(h["k16"], g)) for h, g in probs]
    log_ds = [jnp.where(causal, _rows(h["b_c"], g) - _lanes(h["b_r"], g) + _lanes(h["i_r"], g), -jnp.inf)
              for h, g in probs]
    row_max = [jnp.max(ld, axis=1, keepdims=True) for ld in log_ds]
    m_ts = [jnp.maximum(_rows(h["m_inter"], g), rm) for (h, g), rm in zip(probs, row_max)]
    weights = [qk * jnp.exp(ld - m_t) for qk, ld, m_t in zip(qks, log_ds, m_ts)]
    intras = [_mm(s, _rows(h["v_aug"], g)) for (h, g), s in zip(probs, weights)]
    for hi, h in enumerate(heads):
        h["m_t"] = jnp.concatenate(m_ts[hi * n_groups:(hi + 1) * n_groups], axis=0)
        h["intra"] = jnp.concatenate(intras[hi * n_groups:(hi + 1) * n_groups], axis=0)
    for h in heads:
        h["kv"] = [_mm(h["k_ts"][:, c * CHUNK:(c + 1) * CHUNK], h["v_aug"][c * CHUNK:(c + 1) * CHUNK])
                   for c in range(n_chunks)]

    for hh, h in enumerate(heads):
        state = c_ref[hh]
        inters = []
        for c in range(n_chunks):
            inters.append(_mm(h["q16"][c * CHUNK:(c + 1) * CHUNK], state))
            state = state * h["carry"][:, c * CHUNK:c * CHUNK + 1] + h["kv"][c]
        c_ref[hh] = state
        m_t = h["m_t"]
        num = jnp.exp(h["m_inter"] - m_t) * jnp.concatenate(inters, axis=0) + h["intra"]
        den = jnp.maximum(jnp.abs(num[:, dv:dv + 1]), jnp.exp(-m_t))
        hn = _rms(num[:, 0:dv] / den) * nw_ref[:, hh * dv:(hh + 1) * dv]
        cols = slice(hh * dv, (hh + 1) * dv)
        o_ref[:, cols] = (_sigmoid(og_ref[:, cols].astype(F32)) * hn * _silu(z_ref[:, cols].astype(F32))).astype(BF16)


def _mlstm(p, mcol, mrow, norm_w, batch, seq, tt, hps, col_q, col_k, col_v, col_o, col_z):
    t = batch * seq
    dk, dv = MLSTM_QK_DIM, MLSTM_V_DIM
    qw, vw = hps * dk, hps * dv
    per_seq = seq // tt
    rowi = lambda b, h, i: b * per_seq + i
    return pl.pallas_call(
        functools.partial(_mlstm_kernel, tt=tt, hps=hps),
        grid=(batch, MLSTM_HEADS // hps, per_seq),
        in_specs=[
            pl.BlockSpec((tt, qw), lambda b, h, i: (rowi(b, h, i), col_q // qw + h)),
            pl.BlockSpec((tt, qw), lambda b, h, i: (rowi(b, h, i), col_k // qw + h)),
            pl.BlockSpec((tt, vw), lambda b, h, i: (rowi(b, h, i), col_v // vw + h)),
            pl.BlockSpec((tt, vw), lambda b, h, i: (rowi(b, h, i), col_o // vw + h)),
            pl.BlockSpec((tt, vw), lambda b, h, i: (rowi(b, h, i), col_z // vw + h)),
            pl.BlockSpec((tt, GATE_LANES), lambda b, h, i: (rowi(b, h, i), 0)),
            pl.BlockSpec((None, hps, MLSTM_GATE_SLOTS, tt), lambda b, h, i: (b, h, 0, i)),
            pl.BlockSpec((1, vw), lambda b, h, i: (0, h)),
        ],
        out_specs=pl.BlockSpec((tt, vw), lambda b, h, i: (rowi(b, h, i), h)),
        out_shape=jax.ShapeDtypeStruct((t, MLSTM_HEADS * dv), BF16),
        scratch_shapes=[pltpu.VMEM((hps, dk, dv + 128), F32)],
        compiler_params=_cparams(3),
    )(p, p, p, p, p, mcol, mrow, norm_w)


def _merge_kernel(ya_ref, hb_ref, wa_ref, wb_ref, ga_ref, gb_ref, o_ref):
    ya = jnp.dot(ya_ref[...], wa_ref[...], preferred_element_type=F32)
    yb = jnp.dot(hb_ref[...], wb_ref[...], preferred_element_type=F32)
    merged = _sigmoid(ga_ref[...].astype(F32)) * ya + _sigmoid(gb_ref[...].astype(F32)) * yb
    o_ref[...] = merged.astype(BF16)


def _merge(ya, hb, wa, wb, p, col_ga, col_gb, tm, tn):
    t = ya.shape[0]
    d = wa.shape[1]
    return pl.pallas_call(
        _merge_kernel,
        grid=(t // tm, d // tn),
        in_specs=[pl.BlockSpec((tm, ya.shape[1]), lambda i, j: (i, 0)),
                  pl.BlockSpec((tm, hb.shape[1]), lambda i, j: (i, 0)),
                  pl.BlockSpec((wa.shape[0], tn), lambda i, j: (0, j)),
                  pl.BlockSpec((wb.shape[0], tn), lambda i, j: (0, j)),
                  pl.BlockSpec((tm, tn), lambda i, j: (i, col_ga // tn + j)),
                  pl.BlockSpec((tm, tn), lambda i, j: (i, col_gb // tn + j))],
        out_specs=pl.BlockSpec((tm, tn), lambda i, j: (i, j)),
        out_shape=jax.ShapeDtypeStruct((t, d), BF16),
        compiler_params=_cparams(2),
    )(ya, hb, wa, wb, p, p)


def _out_kernel(mg_ref, w_ref, x_ref, mod_ref, nw_ref, o_ref, *, d):
    out = jnp.dot(mg_ref[...], w_ref[...], preferred_element_type=F32)
    o_ref[...] = x_ref[...] + mod_ref[:, 2 * d:3 * d] * (_rms(out) * nw_ref[...])


def _out(mg, w_out, x2, mod, nw, seq, tm):
    t, d = x2.shape
    per_seq = seq // tm
    return pl.pallas_call(
        functools.partial(_out_kernel, d=d),
        grid=(t // tm,),
        in_specs=[pl.BlockSpec((tm, d), lambda i: (i, 0)),
                  pl.BlockSpec((d, d), lambda i: (0, 0)),
                  pl.BlockSpec((tm, d), lambda i: (i, 0)),
                  pl.BlockSpec((None, 1, 3 * d), lambda i: (i // per_seq, 0, 0)),
                  pl.BlockSpec((1, d), lambda i: (0, 0))],
        out_specs=pl.BlockSpec((tm, d), lambda i: (i, 0)),
        out_shape=jax.ShapeDtypeStruct((t, d), F32),
        compiler_params=_cparams(1),
    )(mg, w_out, x2, mod, nw)


def _gate_columns(d):
    qkv = 2 * GDN_QK_HEADS * GDN_HEAD_DIM + GDN_V_HEADS * GDN_HEAD_DIM
    col_a, col_b = qkv, qkv + GDN_V_HEADS
    col_i = col_b + GDN_V_HEADS + GDN_V_HEADS * GDN_HEAD_DIM + 2 * MLSTM_HEADS * MLSTM_QK_DIM + MLSTM_HEADS * MLSTM_V_DIM
    col_f = col_i + MLSTM_HEADS
    g_src, g_head, g_use = np.zeros(GATE_LANES, np.int32), np.zeros(GATE_LANES, np.int32), np.zeros(GATE_LANES, bool)
    m_src, m_head, m_use = np.zeros(GATE_LANES, np.int32), np.zeros(GATE_LANES, np.int32), np.zeros(GATE_LANES, bool)
    for lane in range(GATE_LANES):
        pair, slot = divmod(lane, GDN_GATE_SLOTS)
        if slot < 6:
            head = 2 * pair + slot % 2
            g_src[lane] = (col_b if slot in (2, 3) else col_a) + head
            g_head[lane], g_use[lane] = head, True
        head, slot = divmod(lane, MLSTM_GATE_SLOTS)
        if slot < 3:
            m_src[lane] = (col_i if slot == 0 else col_f) + head
            m_head[lane], m_use[lane] = head, True
    slot_is_i = (np.arange(GATE_LANES) % MLSTM_GATE_SLOTS) == 0
    return (g_src, g_head, g_use), (m_src, m_head, m_use), slot_is_i


def _layer(x, c, w_ada, b_ada, norm_pre_w, w_in, gdn_conv_w, gdn_A_log, gdn_dt_bias, gdn_norm_w,
           mlstm_conv_w, mlstm_b_i, mlstm_b_f, mlstm_norm_w, w_proj_gdn, w_proj_mlstm, w_out, norm_post_w):
    batch, seq, d = x.shape
    t = batch * seq
    qk_w = GDN_QK_HEADS * GDN_HEAD_DIM
    v_w = GDN_V_HEADS * GDN_HEAD_DIM
    mqk_w = MLSTM_HEADS * MLSTM_QK_DIM
    mv_w = MLSTM_HEADS * MLSTM_V_DIM
    assert d == qk_w == mv_w and seq % 256 == 0

    src_a = 2 * qk_w + v_w
    src_z = src_a + 2 * GDN_V_HEADS
    src_mqk = src_z + v_w
    src_i = src_mqk + 2 * mqk_w + mv_w
    src_o = src_i + 2 * MLSTM_HEADS
    col_q, col_k, col_v, col_z = 0, qk_w, 2 * qk_w, 2 * qk_w + v_w
    col_mq = col_z + v_w
    col_mk, col_mv = col_mq + mqk_w, col_mq + 2 * mqk_w
    col_mo = col_mv + mv_w
    col_mz, col_ga, col_gb = col_mo + mv_w, col_mo + 2 * mv_w, col_mo + 2 * mv_w + d
    n_p = col_gb + d
    w_big = jnp.concatenate([w_in[:, :src_a], w_in[:, src_z:src_i], w_in[:, src_o:]], axis=1).astype(BF16)
    assert w_big.shape[1] == n_p

    tn = 1024
    conv_big = jnp.zeros((8, n_p), F32).at[CONV_WIDTH, :].set(1.0)
    conv_big = conv_big.at[0:CONV_WIDTH, col_q:col_z].set(gdn_conv_w)
    conv_big = conv_big.at[0:CONV_WIDTH, col_mq:col_mv].set(mlstm_conv_w)
    conv_big = conv_big.at[CONV_WIDTH, col_mq:col_mk].set(MLSTM_QK_DIM ** -0.5)
    assert col_z % tn == 0 and col_mq % tn == 0 and col_mv % tn == 0
    conv_tiles = ((col_q // tn, col_z // tn), (col_mq // tn, col_mv // tn))

    (g_src, g_head, g_use), (m_src, m_head, m_use), slot_is_i = _gate_columns(d)
    wg = jnp.where(g_use[None, :], jnp.take(w_in, g_src, axis=1), 0.0)
    wm = jnp.where(m_use[None, :], jnp.take(w_in, m_src, axis=1), 0.0)
    pg = jnp.stack([jnp.where(g_use, jnp.take(gdn_A_log, g_head), 0.0),
                    jnp.where(g_use, jnp.take(gdn_dt_bias, g_head), 0.0)])
    pm = jnp.where(m_use, jnp.where(slot_is_i, jnp.take(mlstm_b_i, m_head), jnp.take(mlstm_b_f, m_head)), 0.0)[None, :]

    x2 = x.reshape(t, d)
    rows = max(8, batch)
    c8 = jnp.pad(c, ((0, rows - batch), (0, 0)))
    mod = _ada(c8, w_ada, b_ada)[:batch].reshape(batch, 1, 3 * d)
    nw_pre = norm_pre_w.reshape(1, d)

    tm = min(1024, seq)
    p = _inproj(x2, mod, nw_pre, w_big, conv_big, seq, tm=tm, tn=tn, conv_tiles=conv_tiles)
    gcol, grow, mcol, mrow = _gates(x2, mod, nw_pre, jnp.concatenate([wg, wm], axis=1), pg, pm, batch, seq,
                                    tg=min(512, seq))
    grow = grow.reshape(batch, GDN_QK_HEADS, GDN_GATE_SLOTS, seq)
    mrow = mrow.reshape(batch, MLSTM_HEADS, MLSTM_GATE_SLOTS, seq)

    tt = 256
    ya = _gdn(p, gcol, grow, gdn_norm_w.reshape(1, GDN_HEAD_DIM), batch, seq, tt, 4,
              col_q, col_k, col_v, col_z)
    hb = _mlstm(p, mcol, mrow, mlstm_norm_w.reshape(1, mv_w), batch, seq, tt, 8,
                col_mq, col_mk, col_mv, col_mo, col_mz)
    mg = _merge(ya, hb, w_proj_gdn.astype(BF16), w_proj_mlstm.astype(BF16), p, col_ga, col_gb, tm=tm, tn=512)
    y = _out(mg, w_out.astype(BF16), x2, mod, norm_post_w.reshape(1, d), seq, tm=min(512, seq))
    return y.reshape(batch, seq, d)


def kernel(x, c, w_ada, b_ada, norm_pre_w, w_in, gdn_conv_w, gdn_A_log, gdn_dt_bias, gdn_norm_w, mlstm_conv_w, mlstm_b_i, mlstm_b_f, mlstm_norm_w, w_proj_gdn, w_proj_mlstm, w_out, norm_post_w):
    for l in range(w_ada.shape[0]):
        x = _layer(x, c, w_ada[l], b_ada[l], norm_pre_w[l], w_in[l], gdn_conv_w[l], gdn_A_log[l],
                   gdn_dt_bias[l], gdn_norm_w[l], mlstm_conv_w[l], mlstm_b_i[l], mlstm_b_f[l],
                   mlstm_norm_w[l], w_proj_gdn[l], w_proj_mlstm[l], w_out[l], norm_post_w[l])
    return x
```

```python
import functools

import jax
import jax.numpy as jnp
import numpy as np
from jax import lax
from jax.experimental import pallas as pl
from jax.experimental.pallas import tpu as pltpu

F32 = jnp.float32
BF16 = jnp.bfloat16

NORM_EPS = 1e-6
CHUNK = 64
CONV_WIDTH = 4
F32_SUBLANES = 8
CONV_HALO = F32_SUBLANES
INV_BASE = 16
GROUP = 128
GDN_WAVE = 4

GDN_QK_HEADS = 16
GDN_V_HEADS = 32
GDN_HEAD_DIM = 128
MLSTM_HEADS = 8
MLSTM_QK_DIM = 128
MLSTM_V_DIM = 256
GATE_LANES = 128
GDN_GATE_SLOTS = GATE_LANES // GDN_QK_HEADS
MLSTM_GATE_SLOTS = GATE_LANES // MLSTM_HEADS

VMEM_LIMIT_BYTES = 56 * 1024 * 1024


def _cparams(n_axes):
    return pltpu.CompilerParams(dimension_semantics=("arbitrary",) * n_axes,
                                vmem_limit_bytes=VMEM_LIMIT_BYTES)


def _mm(a, b):
    return jnp.dot(a.astype(BF16), b.astype(BF16), preferred_element_type=F32)


def _mm_nt(a, b):
    return lax.dot_general(a.astype(BF16), b.astype(BF16), (((1,), (1,)), ((), ())),
                           preferred_element_type=F32)


def _sigmoid(x):
    return 1.0 / (1.0 + jnp.exp(-x))


def _silu(x):
    return x * _sigmoid(x)


def _softplus(x):
    return jnp.maximum(x, 0.0) + jnp.log(1.0 + jnp.exp(-jnp.abs(x)))


def _rms(x):
    return x * lax.rsqrt(jnp.mean(x * x, axis=-1, keepdims=True) + NORM_EPS)


def _prenorm(x_ref, mod_ref, nw_ref, d):
    y = _rms(x_ref[...]) * nw_ref[...]
    return y * (1.0 + mod_ref[:, d:2 * d]) + mod_ref[:, 0:d]


def _ada_kernel(c_ref, w_ref, b_ref, o_ref):
    o_ref[...] = _mm(_silu(c_ref[...]), w_ref[...]) + b_ref[...]


def _ada(c8, w_ada, b_ada, tn=512):
    rows, d = c8.shape
    n = w_ada.shape[1]
    return pl.pallas_call(
        _ada_kernel,
        grid=(n // tn,),
        in_specs=[pl.BlockSpec((rows, d), lambda j: (0, 0)),
                  pl.BlockSpec((d, tn), lambda j: (0, j)),
                  pl.BlockSpec((1, tn), lambda j: (0, j))],
        out_specs=pl.BlockSpec((rows, tn), lambda j: (0, j)),
        out_shape=jax.ShapeDtypeStruct((rows, n), F32),
        compiler_params=_cparams(1),
    )(c8, w_ada, b_ada.reshape(1, n))


def _inproj_kernel(x_ref, mod_ref, nw_ref, w_ref, cw_ref, o_ref, h_ref, tail_ref, *acc_refs,
                   d, tm, sub, per_seq, conv_tiles):
    i, j = pl.program_id(0), pl.program_id(1)

    @pl.when(j == 0)
    def _():
        h_ref[...] = _prenorm(x_ref, mod_ref, nw_ref, d).astype(BF16)

    is_conv = functools.reduce(jnp.logical_or, [(j >= lo) & (j < hi) for lo, hi in conv_tiles])

    def project(r):
        return lax.dot_general(h_ref[r * sub:(r + 1) * sub, :], w_ref[...].astype(BF16),
                               (((1,), (1,)), ((), ())), preferred_element_type=F32)

    @pl.when(jnp.logical_not(is_conv))
    def _():
        for r in range(tm // sub):
            o_ref[r * sub:(r + 1) * sub, :] = project(r).astype(BF16)

    @pl.when(is_conv)
    def _():
        first = (i % per_seq) == 0

        @pl.when(first)
        def _():
            acc_refs[0][0:CONV_HALO, :] = jnp.zeros((CONV_HALO, w_ref.shape[0]), F32)

        @pl.when(jnp.logical_not(first))
        def _():
            acc_refs[0][0:CONV_HALO, :] = tail_ref[j]

        cw = cw_ref[...]
        for r, acc_ref in enumerate(acc_refs):
            acc = project(r)
            acc_ref[CONV_HALO:CONV_HALO + sub, :] = acc
            if r + 1 < len(acc_refs):
                acc_refs[r + 1][0:CONV_HALO, :] = acc[sub - CONV_HALO:sub, :]
            else:
                tail_ref[j] = acc[sub - CONV_HALO:sub, :]
            y = cw[CONV_WIDTH - 1:CONV_WIDTH, :] * acc
            for tap in range(CONV_WIDTH - 1):
                back = CONV_WIDTH - 1 - tap
                y = y + cw[tap:tap + 1, :] * acc_ref[CONV_HALO - back:CONV_HALO - back + sub, :]
            o_ref[r * sub:(r + 1) * sub, :] = (_silu(y) * cw[CONV_WIDTH:CONV_WIDTH + 1, :]).astype(BF16)


def _inproj(x2, mod, nw, w_t, conv_big, seq, tm, tn, conv_tiles, segments):
    t, d = x2.shape
    n = conv_big.shape[1]
    per_seq = seq // tm
    sub = min(256, tm)

    def w_row(j):
        row8, prev = j * (tn // F32_SUBLANES), 0
        for p_col, src_col in segments[1:]:
            delta = src_col - p_col
            assert p_col % tn == 0 and (delta - prev) % F32_SUBLANES == 0
            row8 = row8 + (j >= p_col // tn).astype(jnp.int32) * ((delta - prev) // F32_SUBLANES)
            prev = delta
        return row8 * F32_SUBLANES

    return pl.pallas_call(
        functools.partial(_inproj_kernel, d=d, tm=tm, sub=sub, per_seq=per_seq, conv_tiles=conv_tiles),
        grid=(t // tm, n // tn),
        in_specs=[pl.BlockSpec((tm, d), lambda i, j: (i, 0)),
                  pl.BlockSpec((None, 1, 3 * d), lambda i, j: (i // per_seq, 0, 0)),
                  pl.BlockSpec((1, d), lambda i, j: (0, 0)),
                  pl.BlockSpec((pl.Element(tn), pl.Element(d)), lambda i, j: (w_row(j), 0)),
                  pl.BlockSpec((8, tn), lambda i, j: (0, j))],
        out_specs=pl.BlockSpec((tm, tn), lambda i, j: (i, j)),
        out_shape=jax.ShapeDtypeStruct((t, n), BF16),
        scratch_shapes=[pltpu.VMEM((tm, d), BF16), pltpu.VMEM((n // tn, CONV_HALO, tn), F32)]
        + [pltpu.VMEM((CONV_HALO + sub, tn), F32)] * (tm // sub),
        compiler_params=_cparams(2),
    )(x2, mod, nw, w_t, conv_big)


def _same_block(ri, ci, size):
    shift = size.bit_length() - 1
    return (ri >> shift) == (ci >> shift)


def _gates_kernel(x_ref, mod_ref, nw_ref, whi_ref, wlo_ref, pg_ref, pm_ref,
                  gcol_ref, grow_ref, mcol_ref, mrow_ref, mstate_ref, *, d, tg, per_seq):
    h = _prenorm(x_ref, mod_ref, nw_ref, d)
    h_hi = h.astype(BF16)
    h_lo = (h - h_hi.astype(F32)).astype(BF16)
    w_hi, w_lo = whi_ref[...], wlo_ref[...]
    y = (jnp.dot(h_hi, w_hi, preferred_element_type=F32)
         + (jnp.dot(h_hi, w_lo, preferred_element_type=F32) + jnp.dot(h_lo, w_hi, preferred_element_type=F32)))
    yg, ym = y[:, :GATE_LANES], y[:, GATE_LANES:]
    lane = lax.broadcasted_iota(jnp.int32, (tg, GATE_LANES), 1)

    slot = lane & (GDN_GATE_SLOTS - 1)
    g = -jnp.exp(pg_ref[0:1, :]) * _softplus(yg + pg_ref[1:2, :])
    is_g = (slot == 0) | (slot == 1) | (slot == 4) | (slot == 5)
    is_beta = (slot == 2) | (slot == 3)
    y_g = jnp.where(is_g, g, jnp.where(is_beta, _sigmoid(yg), 0.0))

    mslot = lane & (MLSTM_GATE_SLOTS - 1)
    pre = ym + pm_ref[0:1, :]
    y_m = jnp.where(mslot == 0, pre, jnp.where((mslot == 1) | (mslot == 2), -_softplus(-pre), 0.0))

    ri = lax.broadcasted_iota(jnp.int32, (tg, tg), 0)
    ci = lax.broadcasted_iota(jnp.int32, (tg, tg), 1)
    same = _same_block(ri, ci, CHUNK)
    lower = jnp.where(same & (ri >= ci), 1.0, 0.0).astype(BF16)
    total = jnp.where(same, 1.0, 0.0).astype(BF16)
    y_all = jnp.concatenate([y_g, y_m], axis=1)
    parts = []
    rest = y_all
    for _ in range(3):
        piece = rest.astype(BF16)
        parts.append(piece)
        rest = rest - piece.astype(F32)
    cum = sum(jnp.dot(lower, piece, preferred_element_type=F32) for piece in reversed(parts))
    tot = sum(jnp.dot(total, piece, preferred_element_type=F32) for piece in reversed(parts))

    g_out = jnp.where(slot < 2, cum[:, :GATE_LANES],
                      jnp.where((slot == 4) | (slot == 5), tot[:, :GATE_LANES], y_g))
    m_out = jnp.where(mslot == 1, cum[:, GATE_LANES:],
                      jnp.where(mslot == 2, tot[:, GATE_LANES:], y_m))

    @pl.when(pl.program_id(0) % per_seq == 0)
    def _():
        mstate_ref[...] = jnp.zeros_like(mstate_ref)

    b_al = pltpu.roll(m_out, GATE_LANES - 1, axis=1)
    bl_al = pltpu.roll(m_out, GATE_LANES - 2, axis=1)
    log_end = bl_al - b_al + m_out
    m = mstate_ref[0:1, :]
    m_starts, m_nexts = [], []
    for c in range(tg // CHUNK):
        r0 = c * CHUNK
        m_starts.append(jnp.broadcast_to(m, (CHUNK, GATE_LANES)))
        m = jnp.maximum(bl_al[r0:r0 + 1, :] + m, jnp.max(log_end[r0:r0 + CHUNK, :], axis=0, keepdims=True))
        m_nexts.append(jnp.broadcast_to(m, (CHUNK, GATE_LANES)))
    mstate_ref[...] = jnp.broadcast_to(m, mstate_ref.shape)
    m_out = jnp.where(mslot == 3, pltpu.roll(jnp.concatenate(m_starts, axis=0), 3, axis=1),
                      jnp.where(mslot == 4, pltpu.roll(jnp.concatenate(m_nexts, axis=0), 4, axis=1), m_out))
    gcol_ref[...] = g_out
    grow_ref[...] = g_out.T
    mcol_ref[...] = m_out
    mrow_ref[...] = m_out.T


def _gates(x2, mod, nw, w_gates, pg, pm, batch, seq, tg):
    t, d = x2.shape
    per_seq = seq // tg
    w_hi = w_gates.astype(BF16)
    w_lo = (w_gates - w_hi.astype(F32)).astype(BF16)
    col = jax.ShapeDtypeStruct((t, GATE_LANES), F32)
    row = jax.ShapeDtypeStruct((batch, GATE_LANES, seq), F32)
    col_spec = pl.BlockSpec((tg, GATE_LANES), lambda i: (i, 0))
    row_spec = pl.BlockSpec((None, GATE_LANES, tg), lambda i: (i // per_seq, 0, i % per_seq))
    const = lambda shape: pl.BlockSpec(shape, lambda i: (0, 0))
    return pl.pallas_call(
        functools.partial(_gates_kernel, d=d, tg=tg, per_seq=per_seq),
        grid=(t // tg,),
        in_specs=[pl.BlockSpec((tg, d), lambda i: (i, 0)),
                  pl.BlockSpec((None, 1, 3 * d), lambda i: (i // per_seq, 0, 0)),
                  const((1, d)), const((d, 2 * GATE_LANES)), const((d, 2 * GATE_LANES)),
                  const((2, GATE_LANES)), const((1, GATE_LANES))],
        out_specs=[col_spec, row_spec, col_spec, row_spec],
        out_shape=[col, row, col, row],
        scratch_shapes=[pltpu.VMEM((8, GATE_LANES), F32)],
        compiler_params=_cparams(1),
    )(x2, mod, nw, w_hi, w_lo, pg, pm)


def _unit_lower_inverses(mats, ri, ci):
    diag = _same_block(ri, ci, INV_BASE)
    eye = jnp.where(ri == ci, 1.0, 0.0)
    pows = [jnp.where(diag, a, 0.0) for a in mats]
    invs = [eye - ab for ab in pows]
    order = 2
    while order < INV_BASE:
        pows = [_mm(ab, ab) for ab in pows]
        yield
        invs = [inv + _mm(inv, ab) for inv, ab in zip(invs, pows)]
        yield
        order *= 2
    size = INV_BASE
    while size < CHUNK:
        merged = _same_block(ri, ci, 2 * size)
        keep = merged & jnp.logical_not(diag)
        corr = [_mm(jnp.where(keep, a, 0.0), inv) for a, inv in zip(mats, invs)]
        yield
        invs = [inv - _mm(inv, cr) for inv, cr in zip(invs, corr)]
        yield
        diag, size = merged, 2 * size
    return invs


def _interleave(*gens):
    live = list(gens)
    while live:
        for gen in list(live):
            try:
                next(gen)
            except StopIteration:
                live.remove(gen)


def _lane_slot_shift(group, slots):
    return lax.rem((GATE_LANES // slots - group) * slots, GATE_LANES)


def _group_masks():
    ri = lax.broadcasted_iota(jnp.int32, (GROUP, GROUP), 0)
    ci = lax.broadcasted_iota(jnp.int32, (GROUP, GROUP), 1)
    same = _same_block(ri, ci, CHUNK)
    return ri, ci, same & (ri >= ci), same & (ri > ci)


def _rows(x, g):
    return x[g * GROUP:(g + 1) * GROUP]


def _lanes(x, g):
    return x[:, g * GROUP:(g + 1) * GROUP]


def _gdn_kernel(q_ref, k_ref, v_ref, z_ref, gcol_ref, grow_ref, nw_ref, o_ref, s_ref, *, tt, hps):
    hd = GDN_HEAD_DIM
    n_groups, n_chunks, per_group = tt // GROUP, tt // CHUNK, GROUP // CHUNK

    @pl.when(pl.program_id(2) == 0)
    def _():
        s_ref[...] = jnp.zeros_like(s_ref)

    gcol = pltpu.roll(gcol_ref[...], _lane_slot_shift(pl.program_id(1) * hps, GDN_GATE_SLOTS), axis=1)
    ri, ci, causal, strict = _group_masks()
    n_waves = hps // GDN_WAVE
    waves = [dict(pairs=range(w * GDN_WAVE, (w + 1) * GDN_WAVE), heads=[], a_mats=[], done=[]) for w in range(n_waves)]

    def prep(w):
        return _gdn_prep(waves[w], q_ref, k_ref, v_ref, gcol, grow_ref, causal, strict, n_groups)

    def solve(w):
        return _gdn_solve(waves[w], s_ref, ri, ci, n_groups, n_chunks, per_group)

    def finish(w):
        for vh, state, out in waves[w]["done"]:
            s_ref[vh] = state
            o = _rms(out) * nw_ref[...]
            z = z_ref[:, vh * hd:(vh + 1) * hd].astype(F32)
            o_ref[:, vh * hd:(vh + 1) * hd] = (o * _silu(z)).astype(BF16)
            yield

    _interleave(prep(0))
    for w in range(n_waves):
        _interleave(solve(w), *([prep(w + 1)] if w + 1 < n_waves else []), *([finish(w - 1)] if w > 0 else []))
    _interleave(finish(n_waves - 1))


def _gdn_prep(wave, q_ref, k_ref, v_ref, gcol, grow_ref, causal, strict, n_groups):
    hd = GDN_HEAD_DIM
    heads, a_mats = wave["heads"], wave["a_mats"]
    for pp in wave["pairs"]:
        q = q_ref[:, pp * hd:(pp + 1) * hd].astype(F32)
        k = k_ref[:, pp * hd:(pp + 1) * hd].astype(F32)
        q = q * lax.rsqrt(jnp.sum(q * q, axis=-1, keepdims=True) + NORM_EPS) * (hd ** -0.5)
        k = k * lax.rsqrt(jnp.sum(k * k, axis=-1, keepdims=True) + NORM_EPS)
        q16, k16, k_t = q.astype(BF16), k.astype(BF16), k.T
        kk = [_mm_nt(_rows(k16, g), _rows(k16, g)) for g in range(n_groups)]
        qk = [_mm_nt(_rows(q16, g), _rows(k16, g)) for g in range(n_groups)]
        yield
        for hh in range(2):
            lane = pp * GDN_GATE_SLOTS + hh
            gc_c, be_c = gcol[:, lane:lane + 1], gcol[:, lane + 2:lane + 3]
            gc_r, be_r, gl_r = grow_ref[pp, hh:hh + 1, :], grow_ref[pp, 2 + hh:3 + hh, :], grow_ref[pp, 4 + hh:5 + hh, :]
            decay = [jnp.exp(jnp.where(causal, _rows(gc_c, g) - _lanes(gc_r, g), -jnp.inf)) for g in range(n_groups)]
            a_mats += [jnp.where(strict, _rows(be_c, g) * kk[g] * decay[g], 0.0) for g in range(n_groups)]
            vh = 2 * pp + hh
            heads.append(dict(vh=vh, k16=k16, qk=qk, decay=decay, t_scale=be_r, w_scale=be_r * jnp.exp(gc_r),
                              qd=q * jnp.exp(gc_c), kt_t=(k_t * jnp.exp(gl_r - gc_r)).astype(BF16),
                              g_tot=jnp.exp(gl_r), v=v_ref[:, vh * hd:(vh + 1) * hd]))
            yield


def _gdn_solve(wave, s_ref, ri, ci, n_groups, n_chunks, per_group):
    hd = GDN_HEAD_DIM
    heads = wave["heads"]
    invs = yield from _unit_lower_inverses(wave["a_mats"], ri, ci)

    for hi, h in enumerate(heads):
        h["wu"] = []
        for g in range(n_groups):
            inv = invs[hi * n_groups + g]
            w = _mm(inv * _lanes(h["w_scale"], g), _rows(h["k16"], g))
            u = _mm(inv * _lanes(h["t_scale"], g), _rows(h["v"], g))
            h["wu"].append(jnp.concatenate([w, u], axis=1).astype(BF16))
        yield
    for h in heads:
        h["awu"] = [_mm(h["qk"][g] * h["decay"][g], h["wu"][g]) for g in range(n_groups)]
    yield
    for h in heads:
        h["ktwu"] = []
        for c in range(n_chunks):
            g, r0 = c // per_group, (c % per_group) * CHUNK
            h["ktwu"].append(_mm(h["kt_t"][:, c * CHUNK:(c + 1) * CHUNK], h["wu"][g][r0:r0 + CHUNK]))
        yield

    states = [s_ref[h["vh"]] for h in heads]
    outs = [[] for _ in heads]
    for c in range(n_chunks):
        g, r0 = c // per_group, (c % per_group) * CHUNK
        for hi, h in enumerate(heads):
            aw, au = h["awu"][g][r0:r0 + CHUNK, 0:hd], h["awu"][g][r0:r0 + CHUNK, hd:2 * hd]
            q_eff = h["qd"][c * CHUNK:(c + 1) * CHUNK] - aw
            prod = _mm(jnp.concatenate([h["ktwu"][c][:, 0:hd], q_eff], axis=0), states[hi])
            outs[hi].append(prod[hd:hd + CHUNK] + au)
            states[hi] = (states[hi] * h["g_tot"][:, c * CHUNK:c * CHUNK + 1] - prod[0:hd]
                          + h["ktwu"][c][:, hd:2 * hd])
        yield
    wave["done"].extend((h["vh"], states[hi], jnp.concatenate(outs[hi], axis=0)) for hi, h in enumerate(heads))


def _gdn(p,gcol, grow, norm_w, batch, seq, tt, hps, col_q, col_k, col_v, col_z):
    t = batch * seq
    hd = GDN_HEAD_DIM
    qw, vw = hps * hd, 2 * hps * hd
    per_seq = seq // tt
    rowi = lambda b, h, i: b * per_seq + i
    return pl.pallas_call(
        functools.partial(_gdn_kernel, tt=tt, hps=hps),
        grid=(batch, GDN_QK_HEADS // hps, per_seq),
        in_specs=[
            pl.BlockSpec((tt, qw), lambda b, h, i: (rowi(b, h, i), col_q // qw + h)),
            pl.BlockSpec((tt, qw), lambda b, h, i: (rowi(b, h, i), col_k // qw + h)),
            pl.BlockSpec((tt, vw), lambda b, h, i: (rowi(b, h, i), col_v // vw + h)),
            pl.BlockSpec((tt, vw), lambda b, h, i: (rowi(b, h, i), col_z // vw + h)),
            pl.BlockSpec((tt, GATE_LANES), lambda b, h, i: (rowi(b, h, i), 0)),
            pl.BlockSpec((None, hps, GDN_GATE_SLOTS, tt), lambda b, h, i: (b, h, 0, i)),
            pl.BlockSpec((1, hd), lambda b, h, i: (0, 0)),
        ],
        out_specs=pl.BlockSpec((tt, vw), lambda b, h, i: (rowi(b, h, i), h)),
        out_shape=jax.ShapeDtypeStruct((t, GDN_V_HEADS * hd), BF16),
        scratch_shapes=[pltpu.VMEM((2 * hps, hd, hd), F32)],
        compiler_params=_cparams(3),
    )(p, p, p, p, gcol, grow, norm_w)


def _mlstm_kernel(q_ref, k_ref, v_ref, og_ref, z_ref, mcol_ref, mrow_ref, nw_ref, o_ref, c_ref, *, tt, hps):
    dk, dv = MLSTM_QK_DIM, MLSTM_V_DIM
    n_groups, n_chunks = tt // GROUP, tt // CHUNK

    @pl.when(pl.program_id(2) == 0)
    def _():
        c_ref[...] = jnp.zeros_like(c_ref)

    mcol = pltpu.roll(mcol_ref[...], _lane_slot_shift(pl.program_id(1) * hps, MLSTM_GATE_SLOTS), axis=1)
    _, _, causal, _ = _group_masks()
    lane = lax.broadcasted_iota(jnp.int32, (tt, 128), 1)
    ones_col = jnp.where(lane == 0, 1.0, 0.0).astype(BF16)

    heads = []
    for hh in range(hps):
        q16, k16 = q_ref[:, hh * dk:(hh + 1) * dk], k_ref[:, hh * dk:(hh + 1) * dk]
        v_aug = jnp.concatenate([v_ref[:, hh * dv:(hh + 1) * dv], ones_col], axis=1)
        lane0 = hh * MLSTM_GATE_SLOTS
        b_c, m_start_c = mcol[:, lane0 + 1:lane0 + 2], mcol[:, lane0 + 3:lane0 + 4]
        i_r, b_r, bl_r = mrow_ref[hh, 0:1, :], mrow_ref[hh, 1:2, :], mrow_ref[hh, 2:3, :]
        m_start_r, m_next_r = mrow_ref[hh, 3:4, :], mrow_ref[hh, 4:5, :]
        log_end = bl_r - b_r + i_r
        k_ts =(k16.astype(F32).T * jnp.exp(log_end - m_next_r)).astype(BF16)
        heads.append(dict(q16=q16, k16=k16, v_aug=v_aug, b_c=b_c, b_r=b_r, i_r=i_r, m_inter=b_c + m_start_c,
                          k_ts=k_ts, carry=jnp.exp(bl_r + m_start_r - m_next_r)))

    probs = [(h, g) for h in heads for g in range(n_groups)]
    qks = [_mm_nt(_rows(h["q16"], g), _rows(h["k16"], g)) for h, g in probs]
    log_ds = [jnp.where(causal, _rows(h["b_c"], g) - _lanes(h["b_r"], g) + _lanes(h["i_r"], g), -jnp.inf)
              for h, g in probs]
    row_max = [jnp.max(ld, axis=1, keepdims=True) for ld in log_ds]
    m_ts = [jnp.maximum(_rows(h["m_inter"], g), rm) for (h, g), rm in zip(probs, row_max)]
    weights = [qk * jnp.exp(ld - m_t) for qk, ld, m_t in zip(qks, log_ds, m_ts)]
    intras = [_mm(s, _rows(h["v_aug"], g)) for (h, g), s in zip(probs, weights)]
    for hi, h in enumerate(heads):
        h["m_t"] = jnp.concatenate(m_ts[hi * n_groups:(hi + 1) * n_groups], axis=0)
        h["intra"] = jnp.concatenate(intras[hi * n_groups:(hi + 1) * n_groups], axis=0)
    for h in heads:
        h["kv"] = [_mm(h["k_ts"][:, c * CHUNK:(c + 1) * CHUNK], h["v_aug"][c * CHUNK:(c + 1) * CHUNK])
                   for c in range(n_chunks)]

    for hh, h in enumerate(heads):
        state = c_ref[hh]
        inters = []
        for c in range(n_chunks):
            inters.append(_mm(h["q16"][c * CHUNK:(c + 1) * CHUNK], state))
            state = state * h["carry"][:, c * CHUNK:c * CHUNK + 1] + h["kv"][c]
        c_ref[hh] = state
        m_t = h["m_t"]
        num = jnp.exp(h["m_inter"] - m_t) * jnp.concatenate(inters, axis=0) + h["intra"]
        den = jnp.maximum(jnp.abs(num[:, dv:dv + 1]), jnp.exp(-m_t))
        hn = _rms(num[:, 0:dv] / den) * nw_ref[:, hh * dv:(hh + 1) * dv]
        cols = slice(hh * dv, (hh + 1) * dv)
        o_ref[:, cols] = (_sigmoid(og_ref[:, cols].astype(F32)) * hn * _silu(z_ref[:, cols].astype(F32))).astype(BF16)


def _mlstm(p, mcol, mrow, norm_w, batch, seq, tt, hps, col_q, col_k, col_v, col_o, col_z):
    t = batch * seq
    dk, dv = MLSTM_QK_DIM, MLSTM_V_DIM
    qw, vw = hps * dk, hps * dv
    per_seq = seq // tt
    rowi = lambda b, h, i: b * per_seq + i
    return pl.pallas_call(
        functools.partial(_mlstm_kernel, tt=tt, hps=hps),
        grid=(batch, MLSTM_HEADS // hps, per_seq),
        in_specs=[
            pl.BlockSpec((tt, qw), lambda b, h, i: (rowi(b, h, i), col_q // qw + h)),
            pl.BlockSpec((tt, qw), lambda b, h, i: (rowi(b, h, i), col_k // qw + h)),
            pl.BlockSpec((tt, vw), lambda b, h, i: (rowi(b, h, i), col_v // vw + h)),
            pl.BlockSpec((tt, vw), lambda b, h, i: (rowi(b, h, i), col_o // vw + h)),
            pl.BlockSpec((tt, vw), lambda b, h, i: (rowi(b, h, i), col_z // vw + h)),
            pl.BlockSpec((tt, GATE_LANES), lambda b, h, i: (rowi(b, h, i), 0)),
            pl.BlockSpec((None, hps, MLSTM_GATE_SLOTS, tt), lambda b, h, i: (b, h, 0, i)),
            pl.BlockSpec((1, vw), lambda b, h, i: (0, h)),
        ],
        out_specs=pl.BlockSpec((tt, vw), lambda b, h, i: (rowi(b, h, i), h)),
        out_shape=jax.ShapeDtypeStruct((t, MLSTM_HEADS * dv), BF16),
        scratch_shapes=[pltpu.VMEM((hps, dk, dv + 128), F32)],
        compiler_params=_cparams(3),
    )(p, p, p, p, p, mcol, mrow, norm_w)


def _merge_kernel(ya_ref, hb_ref, wa_ref, wb_ref, ga_ref, gb_ref, o_ref):
    ya = jnp.dot(ya_ref[...], wa_ref[...], preferred_element_type=F32)
    yb = jnp.dot(hb_ref[...], wb_ref[...], preferred_element_type=F32)
    merged = _sigmoid(ga_ref[...].astype(F32)) * ya + _sigmoid(gb_ref[...].astype(F32)) * yb
    o_ref[...] = merged.astype(BF16)


def _merge(ya, hb, wa, wb, p, col_ga, col_gb, tm, tn):
    t = ya.shape[0]
    d = wa.shape[1]
    return pl.pallas_call(
        _merge_kernel,
        grid=(t // tm, d // tn),
        in_specs=[pl.BlockSpec((tm, ya.shape[1]), lambda i, j: (i, 0)),
                  pl.BlockSpec((tm, hb.shape[1]), lambda i, j: (i, 0)),
                  pl.BlockSpec((wa.shape[0], tn), lambda i, j: (0, j)),
                  pl.BlockSpec((wb.shape[0], tn), lambda i, j: (0, j)),
                  pl.BlockSpec((tm, tn), lambda i, j: (i, col_ga // tn + j)),
                  pl.BlockSpec((tm, tn), lambda i, j: (i, col_gb // tn + j))],
        out_specs=pl.BlockSpec((tm, tn), lambda i, j: (i, j)),
        out_shape=jax.ShapeDtypeStruct((t, d), BF16),
        compiler_params=_cparams(2),
    )(ya, hb, wa, wb, p, p)


def _out_kernel(mg_ref, w_ref, x_ref, mod_ref, nw_ref, o_ref, *, d):
    out = jnp.dot(mg_ref[...], w_ref[...], preferred_element_type=F32)
    o_ref[...] = x_ref[...] + mod_ref[:, 2 * d:3 * d] * (_rms(out) * nw_ref[...])


def _out(mg, w_out, x2, mod, nw, seq, tm):
    t, d = x2.shape
    per_seq = seq // tm
    return pl.pallas_call(
        functools.partial(_out_kernel, d=d),
        grid=(t // tm,),
        in_specs=[pl.BlockSpec((tm, d), lambda i: (i, 0)),
                  pl.BlockSpec((d, d), lambda i: (0, 0)),
                  pl.BlockSpec((tm, d), lambda i: (i, 0)),
                  pl.BlockSpec((None, 1, 3 * d), lambda i: (i // per_seq, 0, 0)),
                  pl.BlockSpec((1, d), lambda i: (0, 0))],
        out_specs=pl.BlockSpec((tm, d), lambda i: (i, 0)),
        out_shape=jax.ShapeDtypeStruct((t, d), F32),
        compiler_params=_cparams(1),
    )(mg, w_out, x2, mod, nw)


def _gate_columns(d):
    qkv = 2 * GDN_QK_HEADS * GDN_HEAD_DIM + GDN_V_HEADS * GDN_HEAD_DIM
    col_a, col_b = qkv, qkv + GDN_V_HEADS
    col_i = col_b + GDN_V_HEADS + GDN_V_HEADS * GDN_HEAD_DIM + 2 * MLSTM_HEADS * MLSTM_QK_DIM + MLSTM_HEADS * MLSTM_V_DIM
    col_f = col_i + MLSTM_HEADS
    g_src, g_head, g_use = np.zeros(GATE_LANES, np.int32), np.zeros(GATE_LANES, np.int32), np.zeros(GATE_LANES, bool)
    m_src, m_head, m_use = np.zeros(GATE_LANES, np.int32), np.zeros(GATE_LANES, np.int32), np.zeros(GATE_LANES, bool)
    for lane in range(GATE_LANES):
        pair, slot = divmod(lane, GDN_GATE_SLOTS)
        if slot < 6:
            head = 2 * pair + slot % 2
            g_src[lane] = (col_b if slot in (2, 3) else col_a) + head
            g_head[lane], g_use[lane] = head, True
        head, slot = divmod(lane, MLSTM_GATE_SLOTS)
        if slot < 3:
            m_src[lane] = (col_i if slot == 0 else col_f) + head
            m_head[lane], m_use[lane] = head, True
    slot_is_i = (np.arange(GATE_LANES) % MLSTM_GATE_SLOTS) == 0
    return (g_src, g_head, g_use), (m_src, m_head, m_use), slot_is_i


def _layer(x, c, w_ada, b_ada, norm_pre_w, w_in, gdn_conv_w, gdn_A_log, gdn_dt_bias, gdn_norm_w,
           mlstm_conv_w, mlstm_b_i, mlstm_b_f, mlstm_norm_w, w_proj_gdn, w_proj_mlstm, w_out, norm_post_w):
    batch, seq, d = x.shape
    t = batch * seq
    qk_w = GDN_QK_HEADS * GDN_HEAD_DIM
    v_w = GDN_V_HEADS * GDN_HEAD_DIM
    mqk_w = MLSTM_HEADS * MLSTM_QK_DIM
    mv_w = MLSTM_HEADS * MLSTM_V_DIM
    assert d == qk_w == mv_w and seq % 256 == 0

    src_a = 2 * qk_w + v_w
    src_z = src_a + 2 * GDN_V_HEADS
    src_mqk = src_z + v_w
    src_i = src_mqk + 2 * mqk_w + mv_w
    src_o = src_i + 2 * MLSTM_HEADS
    col_q, col_k, col_v, col_z = 0, qk_w, 2 * qk_w, 2 * qk_w + v_w
    col_mq = col_z + v_w
    col_mk, col_mv = col_mq + mqk_w, col_mq + 2 * mqk_w
    col_mo = col_mv + mv_w
    col_mz, col_ga, col_gb = col_mo + mv_w, col_mo + 2 * mv_w, col_mo + 2 * mv_w + d
    n_p = col_gb + d
    tn = 1024
    segments = ((0, 0), (col_z, src_z), (col_mo, src_o))
    assert all(p_col % tn == 0 for p_col, _ in segments) and n_p - col_mo == w_in.shape[1] - src_o
    w_t = jnp.swapaxes(w_in, 0, 1)

    conv_big = jnp.zeros((8, n_p), F32).at[CONV_WIDTH, :].set(1.0)
    conv_big = conv_big.at[0:CONV_WIDTH, col_q:col_z].set(gdn_conv_w)
    conv_big = conv_big.at[0:CONV_WIDTH, col_mq:col_mv].set(mlstm_conv_w)
    conv_big = conv_big.at[CONV_WIDTH, col_mq:col_mk].set(MLSTM_QK_DIM ** -0.5)
    assert col_z % tn == 0 and col_mq % tn == 0 and col_mv % tn == 0
    conv_tiles = ((col_q // tn, col_z // tn), (col_mq // tn, col_mv // tn))

    (g_src, g_head, g_use), (m_src, m_head, m_use), slot_is_i = _gate_columns(d)
    wg = jnp.where(g_use[None, :], jnp.take(w_t, g_src, axis=0).T, 0.0)
    wm = jnp.where(m_use[None, :], jnp.take(w_t, m_src, axis=0).T, 0.0)
    pg = jnp.stack([jnp.where(g_use, jnp.take(gdn_A_log, g_head), 0.0),
                    jnp.where(g_use, jnp.take(gdn_dt_bias, g_head), 0.0)])
    pm = jnp.where(m_use, jnp.where(slot_is_i, jnp.take(mlstm_b_i, m_head), jnp.take(mlstm_b_f, m_head)), 0.0)[None, :]

    x2 = x.reshape(t, d)
    rows = max(8, batch)
    c8 = jnp.pad(c, ((0, rows - batch), (0, 0)))
    mod = _ada(c8, w_ada, b_ada)[:batch].reshape(batch, 1, 3 * d)
    nw_pre = norm_pre_w.reshape(1, d)

    tm = min(1024, seq)
    p = _inproj(x2, mod, nw_pre, w_t, conv_big, seq, tm=tm, tn=tn, conv_tiles=conv_tiles, segments=segments)
    gcol, grow, mcol, mrow = _gates(x2, mod, nw_pre, jnp.concatenate([wg, wm], axis=1), pg, pm, batch, seq,
                                    tg=min(512, seq))
    grow = grow.reshape(batch, GDN_QK_HEADS, GDN_GATE_SLOTS, seq)
    mrow = mrow.reshape(batch, MLSTM_HEADS, MLSTM_GATE_SLOTS, seq)

    tt = 256
    ya = _gdn(p, gcol, grow, gdn_norm_w.reshape(1, GDN_HEAD_DIM), batch, seq, tt, 8,
              col_q, col_k, col_v, col_z)
    hb = _mlstm(p, mcol, mrow, mlstm_norm_w.reshape(1, mv_w), batch, seq, tt, 8,
                col_mq, col_mk, col_mv, col_mo, col_mz)
    mg = _merge(ya, hb, w_proj_gdn.astype(BF16), w_proj_mlstm.astype(BF16), p, col_ga, col_gb, tm=tm, tn=512)
    y = _out(mg, w_out.astype(BF16), x2, mod, norm_post_w.reshape(1, d), seq, tm=min(512, seq))
    return y.reshape(batch, seq, d)


def kernel(x, c, w_ada, b_ada, norm_pre_w, w_in, gdn_conv_w, gdn_A_log, gdn_dt_bias, gdn_norm_w, mlstm_conv_w, mlstm_b_i, mlstm_b_f, mlstm_norm_w, w_proj_gdn, w_proj_mlstm, w_out, norm_post_w):
    for l in range(w_ada.shape[0]):
        x = _layer(x, c, w_ada[l], b_ada[l], norm_pre_w[l], w_in[l], gdn_conv_w[l], gdn_A_log[l],
                   gdn_dt_bias[l], gdn_norm_w[l], mlstm_conv_w[l], mlstm_b_i[l], mlstm_b_f[l],
                   mlstm_norm_w[l], w_proj_gdn[l], w_proj_mlstm[l], w_out[l], norm_post_w[l])
    return x
```

```python
import functools

import jax
import jax.numpy as jnp
import numpy as np
from jax import lax
from jax.experimental import pallas as pl
from jax.experimental.pallas import tpu as pltpu

F32 = jnp.float32
BF16 = jnp.bfloat16

NORM_EPS = 1e-6
CHUNK = 64
CONV_WIDTH = 4
LANES = 128
F32_SUBLANES = 8
BF16_SUBLANES = 16
CONV_BLOCK = 64
INV_BASE = 8
GROUP = 128
GDN_WAVE = 4

GDN_QK_HEADS = 16
GDN_V_HEADS = 32
GDN_HEAD_DIM = 128
MLSTM_HEADS = 8
MLSTM_QK_DIM = 128
MLSTM_V_DIM = 256
GATE_LANES = 128
GDN_GATE_SLOTS = GATE_LANES // GDN_QK_HEADS
MLSTM_GATE_SLOTS = GATE_LANES // MLSTM_HEADS

VMEM_LIMIT_BYTES = 56 * 1024 * 1024


def _cparams(n_axes):
    return pltpu.CompilerParams(dimension_semantics=("arbitrary",) * n_axes,
                                vmem_limit_bytes=VMEM_LIMIT_BYTES)


def _mm(a, b):
    return jnp.dot(a.astype(BF16), b.astype(BF16), preferred_element_type=F32)


def _mm_nt(a, b):
    return lax.dot_general(a.astype(BF16), b.astype(BF16), (((1,), (1,)), ((), ())),
                           preferred_element_type=F32)


def _sigmoid(x):
    return 0.5 + 0.5 * jnp.tanh(0.5 * x)


def _silu(x):
    half = 0.5 * x
    return half + half * jnp.tanh(half)


def _softplus(x):
    return jnp.maximum(x, 0.0) + jnp.log(1.0 + jnp.exp(-jnp.abs(x)))


def _rms(x):
    return x * lax.rsqrt(jnp.mean(x * x, axis=-1, keepdims=True) + NORM_EPS)


def _prenorm(x, mod_ref, nw_ref, d):
    y = _rms(x) * nw_ref[...]
    return y * (1.0 + mod_ref[:, d:2 * d]) + mod_ref[:, 0:d]


def _ada_kernel(c_ref, w_ref, b_ref, o_ref):
    o_ref[...] = _mm(_silu(c_ref[...]), w_ref[...]) + b_ref[...]


def _ada(c8, w_ada, b_ada, tn=512):
    rows, d = c8.shape
    n = w_ada.shape[1]
    return pl.pallas_call(
        _ada_kernel,
        grid=(n // tn,),
        in_specs=[pl.BlockSpec((rows, d), lambda j: (0, 0)),
                  pl.BlockSpec((d, tn), lambda j: (0, j)),
                  pl.BlockSpec((1, tn), lambda j: (0, j))],
        out_specs=pl.BlockSpec((rows, tn), lambda j: (0, j)),
        out_shape=jax.ShapeDtypeStruct((rows, n), F32),
        compiler_params=_cparams(1),
    )(c8, w_ada, b_ada.reshape(1, n))


def _inproj_kernel(x_ref, mod_ref, nw_ref, w_ref, cw_ref, o_ref, h_ref, hperm_ref, tail_ref, unperm_ref, stage_ref,
                   *, d, tm, sub, per_seq, conv_tiles):
    i, j = pl.program_id(0), pl.program_id(1)
    tn = o_ref.shape[1]
    slab = F32_SUBLANES
    n_slabs = CONV_BLOCK // slab

    @pl.when(j == 0)
    def _():
        def block(blk, carry):
            base = pl.multiple_of(blk * CONV_BLOCK, CONV_BLOCK)
            hb = _prenorm(x_ref[pl.ds(base, CONV_BLOCK), :], mod_ref, nw_ref, d)
            h_ref[pl.ds(base, CONV_BLOCK), :] = hb.astype(BF16)
            for lt in range(d // LANES):
                stage_ref[lt] = hb[:, lt * LANES:(lt + 1) * LANES]
            for s in range(0, n_slabs, 2):
                rows = jnp.concatenate(
                    [jnp.concatenate([stage_ref[lt, pl.ds(s + e, slab, stride=n_slabs), :]
                                      for lt in range(d // LANES)], axis=1) for e in range(2)], axis=0)
                hperm_ref[pl.ds(base + slab * s, 2 * slab), :] = rows.astype(BF16)
            return carry

        lax.fori_loop(0, tm // CONV_BLOCK, block, 0)

    is_conv = functools.reduce(jnp.logical_or, [(j >= lo) & (j < hi) for lo, hi in conv_tiles])

    def project(lhs_ref, r):
        return lax.dot_general(lhs_ref[r * sub:(r + 1) * sub, :], w_ref[...].astype(BF16),
                               (((1,), (1,)), ((), ())), preferred_element_type=F32)

    @pl.when(jnp.logical_not(is_conv))
    def _():
        for r in range(tm // sub):
            o_ref[r * sub:(r + 1) * sub, :] = project(h_ref, r).astype(BF16)

    @pl.when(is_conv)
    def _():
        @pl.when((i % per_seq) == 0)
        def _():
            tail_ref[j] = jnp.zeros(tail_ref.shape[1:], F32)

        cw = cw_ref[...]
        top = lax.broadcasted_iota(jnp.int32, (slab, tn), 0) == 0
        keep = n_slabs - (CONV_WIDTH - 1)
        prev = [pltpu.roll(tail_ref[j, k * slab:(k + 1) * slab, :], 1, axis=0) for k in range(CONV_WIDTH - 1)]
        last = None
        for r in range(tm // sub):
            acc = project(hperm_ref, r)
            for blk in range(sub // CONV_BLOCK):
                row0 = blk * CONV_BLOCK
                slabs = [acc[row0 + s * slab:row0 + (s + 1) * slab, :] for s in range(n_slabs)]
                moved = [pltpu.roll(slabs[keep + k], 1, axis=0) for k in range(CONV_WIDTH - 1)]
                wrapped = [jnp.where(top, prev[k], moved[k]) for k in range(CONV_WIDTH - 1)]
                prev, last = moved, slabs
                for s in range(n_slabs):
                    half = cw[CONV_WIDTH - 1:CONV_WIDTH, :] * slabs[s]
                    for back in range(1, CONV_WIDTH):
                        src = slabs[s - back] if s >= back else wrapped[s - back + n_slabs - keep]
                        half = half + cw[CONV_WIDTH - 1 - back:CONV_WIDTH - back, :] * src
                    out = half + half * jnp.tanh(half)
                    for lt in range(tn // LANES):
                        unperm_ref[lt, pl.ds(row0 + s, slab, stride=n_slabs), :] = out[:, lt * LANES:(lt + 1) * LANES]
            for lt in range(tn // LANES):
                o_ref[r * sub:(r + 1) * sub, lt * LANES:(lt + 1) * LANES] = unperm_ref[lt].astype(BF16)
        tail_ref[j] = jnp.concatenate(last[keep:], axis=0)


def _inproj(x2, mod, nw, w_t, conv_big, seq, tm, tn, conv_tiles, segments):
    t, d = x2.shape
    n = conv_big.shape[1]
    per_seq = seq // tm
    sub = min(256, tm)

    def w_row(j):
        units, prev = j * (tn // BF16_SUBLANES), 0
        for p_col, src_col in segments[1:]:
            delta = src_col - p_col
            assert p_col % tn == 0 and (delta - prev) % BF16_SUBLANES == 0
            units = units + (j >= p_col // tn).astype(jnp.int32) * ((delta - prev) // BF16_SUBLANES)
            prev = delta
        return units * BF16_SUBLANES

    return pl.pallas_call(
        functools.partial(_inproj_kernel, d=d, tm=tm, sub=sub, per_seq=per_seq, conv_tiles=conv_tiles),
        grid=(t // tm, n // tn),
        in_specs=[pl.BlockSpec((tm, d), lambda i, j: (i, 0)),
                  pl.BlockSpec((None, 1, 3 * d), lambda i, j: (i // per_seq, 0, 0)),
                  pl.BlockSpec((1, d), lambda i, j: (0, 0)),
                  pl.BlockSpec((pl.Element(tn), pl.Element(d)), lambda i, j: (w_row(j), 0)),
                  pl.BlockSpec((8, tn), lambda i, j: (0, j))],
        out_specs=pl.BlockSpec((tm, tn), lambda i, j: (i, j)),
        out_shape=jax.ShapeDtypeStruct((t, n), BF16),
        scratch_shapes=[pltpu.VMEM((tm, d), BF16), pltpu.VMEM((tm, d), BF16),
                        pltpu.VMEM((n // tn, (CONV_WIDTH - 1) * F32_SUBLANES, tn), F32),
                        pltpu.VMEM((tn // LANES, sub, LANES), F32), pltpu.VMEM((d // LANES, CONV_BLOCK, LANES), F32)],
        compiler_params=_cparams(2),
    )(x2, mod, nw, w_t, conv_big)


def _same_block(ri, ci, size):
    shift = size.bit_length() - 1
    return (ri >> shift) == (ci >> shift)


def _gates_kernel(x_ref, mod_ref, nw_ref, whi_ref, wlo_ref, pg_ref, pm_ref,
                  gcol_ref, grow_ref, mcol_ref, mrow_ref, mstate_ref, *, d, tg, per_seq):
    h = _prenorm(x_ref[...], mod_ref, nw_ref, d)
    h_hi = h.astype(BF16)
    h_lo = (h - h_hi.astype(F32)).astype(BF16)
    w_hi, w_lo = whi_ref[...], wlo_ref[...]
    y = (jnp.dot(h_hi, w_hi, preferred_element_type=F32)
         + (jnp.dot(h_hi, w_lo, preferred_element_type=F32) + jnp.dot(h_lo, w_hi, preferred_element_type=F32)))
    yg, ym = y[:, :GATE_LANES], y[:, GATE_LANES:]
    lane = lax.broadcasted_iota(jnp.int32, (tg, GATE_LANES), 1)

    slot = lane & (GDN_GATE_SLOTS - 1)
    g = -jnp.exp(pg_ref[0:1, :]) * _softplus(yg + pg_ref[1:2, :])
    is_g = (slot == 0) | (slot == 1) | (slot == 4) | (slot == 5)
    is_beta = (slot == 2) | (slot == 3)
    y_g = jnp.where(is_g, g, jnp.where(is_beta, _sigmoid(yg), 0.0))

    mslot = lane & (MLSTM_GATE_SLOTS - 1)
    pre = ym + pm_ref[0:1, :]
    y_m = jnp.where(mslot == 0, pre, jnp.where((mslot == 1) | (mslot == 2), -_softplus(-pre), 0.0))

    ri = lax.broadcasted_iota(jnp.int32, (tg, tg), 0)
    ci = lax.broadcasted_iota(jnp.int32, (tg, tg), 1)
    same = _same_block(ri, ci, CHUNK)
    lower = jnp.where(same & (ri >= ci), 1.0, 0.0).astype(BF16)
    total = jnp.where(same, 1.0, 0.0).astype(BF16)
    y_all = jnp.concatenate([y_g, y_m], axis=1)
    parts = []
    rest = y_all
    for _ in range(3):
        piece = rest.astype(BF16)
        parts.append(piece)
        rest = rest - piece.astype(F32)
    cum = sum(jnp.dot(lower, piece, preferred_element_type=F32) for piece in reversed(parts))
    tot = sum(jnp.dot(total, piece, preferred_element_type=F32) for piece in reversed(parts))

    g_out = jnp.where(slot < 2, cum[:, :GATE_LANES],
                      jnp.where((slot == 4) | (slot == 5), tot[:, :GATE_LANES], y_g))
    m_out = jnp.where(mslot == 1, cum[:, GATE_LANES:],
                      jnp.where(mslot == 2, tot[:, GATE_LANES:], y_m))

    @pl.when(pl.program_id(0) % per_seq == 0)
    def _():
        mstate_ref[...] = jnp.zeros_like(mstate_ref)

    b_al = pltpu.roll(m_out, GATE_LANES - 1, axis=1)
    bl_al = pltpu.roll(m_out, GATE_LANES - 2, axis=1)
    log_end = bl_al - b_al + m_out
    m = mstate_ref[0:1, :]
    m_starts, m_nexts = [], []
    for c in range(tg // CHUNK):
        r0 = c * CHUNK
        m_starts.append(jnp.broadcast_to(m, (CHUNK, GATE_LANES)))
        m = jnp.maximum(bl_al[r0:r0 + 1, :] + m, jnp.max(log_end[r0:r0 + CHUNK, :], axis=0, keepdims=True))
        m_nexts.append(jnp.broadcast_to(m, (CHUNK, GATE_LANES)))
    mstate_ref[...] = jnp.broadcast_to(m, mstate_ref.shape)
    m_out = jnp.where(mslot == 3, pltpu.roll(jnp.concatenate(m_starts, axis=0), 3, axis=1),
                      jnp.where(mslot == 4, pltpu.roll(jnp.concatenate(m_nexts, axis=0), 4, axis=1), m_out))
    gcol_ref[...] = g_out
    grow_ref[...] = g_out.T
    mcol_ref[...] = m_out
    mrow_ref[...] = m_out.T


def _gates(x2, mod, nw, w_gates, pg, pm, batch, seq, tg):
    t, d = x2.shape
    per_seq = seq // tg
    w_hi = w_gates.astype(BF16)
    w_lo = (w_gates - w_hi.astype(F32)).astype(BF16)
    col = jax.ShapeDtypeStruct((t, GATE_LANES), F32)
    row = jax.ShapeDtypeStruct((batch, GATE_LANES, seq), F32)
    col_spec = pl.BlockSpec((tg, GATE_LANES), lambda i: (i, 0))
    row_spec = pl.BlockSpec((None, GATE_LANES, tg), lambda i: (i // per_seq, 0, i % per_seq))
    const = lambda shape: pl.BlockSpec(shape, lambda i: (0, 0))
    return pl.pallas_call(
        functools.partial(_gates_kernel, d=d, tg=tg, per_seq=per_seq),
        grid=(t // tg,),
        in_specs=[pl.BlockSpec((tg, d), lambda i: (i, 0)),
                  pl.BlockSpec((None, 1, 3 * d), lambda i: (i // per_seq, 0, 0)),
                  const((1, d)), const((d, 2 * GATE_LANES)), const((d, 2 * GATE_LANES)),
                  const((2, GATE_LANES)), const((1, GATE_LANES))],
        out_specs=[col_spec, row_spec, col_spec, row_spec],
        out_shape=[col, row, col, row],
        scratch_shapes=[pltpu.VMEM((8, GATE_LANES), F32)],
        compiler_params=_cparams(1),
    )(x2, mod, nw, w_hi, w_lo, pg, pm)


def _unit_lower_inverses(mats, ri, ci):
    diag = _same_block(ri, ci, INV_BASE)
    eye = jnp.where(ri == ci, 1.0, 0.0)
    pows = [jnp.where(diag, a, 0.0) for a in mats]
    invs = [eye - ab for ab in pows]
    order = 2
    while order < INV_BASE:
        pows = [_mm(ab, ab) for ab in pows]
        yield
        invs = [inv + _mm(inv, ab) for inv, ab in zip(invs, pows)]
        yield
        order *= 2
    size = INV_BASE
    while size < CHUNK:
        merged = _same_block(ri, ci, 2 * size)
        keep = merged & jnp.logical_not(diag)
        corr = [_mm(_odd_blocks(jnp.where(keep, a, 0.0), size), inv) for a, inv in zip(mats, invs)]
        yield
        zeros = jnp.zeros_like(invs[0])
        invs = [_with_odd_blocks(inv, _odd_blocks(inv, size) - _mm(_odd_blocks(inv, size),
                                                                     _with_odd_blocks(zeros, cr, size)), size)
                for inv, cr in zip(invs, corr)]
        yield
        diag, size = merged, 2 * size
    return invs


def _odd_blocks(x, size):
    return jnp.concatenate([x[b * size:(b + 1) * size] for b in range(1, x.shape[0] // size, 2)], axis=0)


def _with_odd_blocks(x, odd, size):
    blocks = [odd[(b // 2) * size:(b // 2 + 1) * size] if b % 2 else x[b * size:(b + 1) * size]
              for b in range(x.shape[0] // size)]
    return jnp.concatenate(blocks, axis=0)


def _interleave(*gens):
    live = list(gens)
    while live:
        for gen in list(live):
            try:
                next(gen)
            except StopIteration:
                live.remove(gen)


def _lane_slot_shift(group, slots):
    return lax.rem((GATE_LANES // slots - group) * slots, GATE_LANES)


def _group_masks():
    ri = lax.broadcasted_iota(jnp.int32, (GROUP, GROUP), 0)
    ci = lax.broadcasted_iota(jnp.int32, (GROUP, GROUP), 1)
    same = _same_block(ri, ci, CHUNK)
    return ri, ci, same & (ri >= ci), same & (ri > ci)


def _rows(x, g):
    return x[g * GROUP:(g + 1) * GROUP]


def _lanes(x, g):
    return x[:, g * GROUP:(g + 1) * GROUP]


def _gdn_kernel(q_ref, k_ref, v_ref, z_ref, gcol_ref, grow_ref, nw_ref, o_ref, s_ref, *, tt, hps):
    hd = GDN_HEAD_DIM
    n_groups, n_chunks, per_group = tt // GROUP, tt // CHUNK, GROUP // CHUNK

    @pl.when(pl.program_id(2) == 0)
    def _():
        s_ref[...] = jnp.zeros_like(s_ref)

    gcol = pltpu.roll(gcol_ref[...], _lane_slot_shift(pl.program_id(1) * hps, GDN_GATE_SLOTS), axis=1)
    ri, ci, causal, strict = _group_masks()
    n_waves = hps // GDN_WAVE
    waves = [dict(pairs=range(w * GDN_WAVE, (w + 1) * GDN_WAVE), heads=[], a_mats=[], done=[]) for w in range(n_waves)]

    def prep(w):
        return _gdn_prep(waves[w], q_ref, k_ref, v_ref, gcol, grow_ref, causal, strict, n_groups)

    def solve(w):
        return _gdn_solve(waves[w], s_ref, ri, ci, n_groups, n_chunks, per_group)

    def finish(w):
        for vh, state, out in waves[w]["done"]:
            s_ref[vh] = state
            o = _rms(out) * nw_ref[...]
            z = z_ref[:, vh * hd:(vh + 1) * hd].astype(F32)
            o_ref[:, vh * hd:(vh + 1) * hd] = (o * _silu(z)).astype(BF16)
            yield

    _interleave(prep(0))
    for w in range(n_waves):
        _interleave(solve(w), *([prep(w + 1)] if w + 1 < n_waves else []), *([finish(w - 1)] if w > 0 else []))
    _interleave(finish(n_waves - 1))


def _gdn_prep(wave, q_ref, k_ref, v_ref, gcol, grow_ref, causal, strict, n_groups):
    hd = GDN_HEAD_DIM
    heads, a_mats = wave["heads"], wave["a_mats"]
    for pp in wave["pairs"]:
        q = q_ref[:, pp * hd:(pp + 1) * hd].astype(F32)
        k = k_ref[:, pp * hd:(pp + 1) * hd].astype(F32)
        q = q * lax.rsqrt(jnp.sum(q * q, axis=-1, keepdims=True) + NORM_EPS) * (hd ** -0.5)
        k = k * lax.rsqrt(jnp.sum(k * k, axis=-1, keepdims=True) + NORM_EPS)
        q16, k16, k_t = q.astype(BF16), k.astype(BF16), k.T
        kk = [_mm_nt(_rows(k16, g), _rows(k16, g)) for g in range(n_groups)]
        qk = [_mm_nt(_rows(q16, g), _rows(k16, g)) for g in range(n_groups)]
        yield
        for hh in range(2):
            lane = pp * GDN_GATE_SLOTS + hh
            gc_c, be_c = gcol[:, lane:lane + 1], gcol[:, lane + 2:lane + 3]
            gc_r, be_r, gl_r = grow_ref[pp, hh:hh + 1, :], grow_ref[pp, 2 + hh:3 + hh, :], grow_ref[pp, 4 + hh:5 + hh, :]
            decay = [jnp.exp(jnp.where(causal, _rows(gc_c, g) - _lanes(gc_r, g), -jnp.inf)) for g in range(n_groups)]
            a_mats += [jnp.where(strict, _rows(be_c, g) * kk[g] * decay[g], 0.0) for g in range(n_groups)]
            vh = 2 * pp + hh
            heads.append(dict(vh=vh, k16=k16, qk=qk, decay=decay, t_scale=be_r, w_scale=be_r * jnp.exp(gc_r),
                              qd=q * jnp.exp(gc_c), kt_t=(k_t * jnp.exp(gl_r - gc_r)).astype(BF16),
                              g_tot=jnp.exp(gl_r), v=v_ref[:, vh * hd:(vh + 1) * hd]))
            yield


def _gdn_solve(wave, s_ref, ri, ci, n_groups, n_chunks, per_group):
    hd = GDN_HEAD_DIM
    heads = wave["heads"]
    invs = yield from _unit_lower_inverses(wave["a_mats"], ri, ci)

    for hi, h in enumerate(heads):
        h["wu"] = []
        for g in range(n_groups):
            inv = invs[hi * n_groups + g]
            w = _mm(inv * _lanes(h["w_scale"], g), _rows(h["k16"], g))
            u = _mm(inv * _lanes(h["t_scale"], g), _rows(h["v"], g))
            h["wu"].append(jnp.concatenate([w, u], axis=1).astype(BF16))
        yield
    for h in heads:
        h["awu"] = [_mm(h["qk"][g] * h["decay"][g], h["wu"][g]) for g in range(n_groups)]
    yield
    for h in heads:
        h["ktwu"] = []
        for c in range(n_chunks):
            g, r0 = c // per_group, (c % per_group) * CHUNK
            h["ktwu"].append(_mm(h["kt_t"][:, c * CHUNK:(c + 1) * CHUNK], h["wu"][g][r0:r0 + CHUNK]))
        yield

    states = [s_ref[h["vh"]] for h in heads]
    outs = [[] for _ in heads]
    for c in range(n_chunks):
        g, r0 = c // per_group, (c % per_group) * CHUNK
        for hi, h in enumerate(heads):
            aw, au = h["awu"][g][r0:r0 + CHUNK, 0:hd], h["awu"][g][r0:r0 + CHUNK, hd:2 * hd]
            q_eff = h["qd"][c * CHUNK:(c + 1) * CHUNK] - aw
            prod = _mm(jnp.concatenate([h["ktwu"][c][:, 0:hd], q_eff], axis=0), states[hi])
            outs[hi].append(prod[hd:hd + CHUNK] + au)
            states[hi] = (states[hi] * h["g_tot"][:, c * CHUNK:c * CHUNK + 1] - prod[0:hd]
                          + h["ktwu"][c][:, hd:2 * hd])
        yield
    wave["done"].extend((h["vh"], states[hi], jnp.concatenate(outs[hi], axis=0)) for hi, h in enumerate(heads))


def _gdn(p,gcol, grow, norm_w, batch, seq, tt, hps, col_q, col_k, col_v, col_z):
    t = batch * seq
    hd = GDN_HEAD_DIM
    qw, vw = hps * hd, 2 * hps * hd
    per_seq = seq // tt
    rowi = lambda b, h, i: b * per_seq + i
    return pl.pallas_call(
        functools.partial(_gdn_kernel, tt=tt, hps=hps),
        grid=(batch, GDN_QK_HEADS // hps, per_seq),
        in_specs=[
            pl.BlockSpec((tt, qw), lambda b, h, i: (rowi(b, h, i), col_q // qw + h)),
            pl.BlockSpec((tt, qw), lambda b, h, i: (rowi(b, h, i), col_k // qw + h)),
            pl.BlockSpec((tt, vw), lambda b, h, i: (rowi(b, h, i), col_v // vw + h)),
            pl.BlockSpec((tt, vw), lambda b, h, i: (rowi(b, h, i), col_z // vw + h)),
            pl.BlockSpec((tt, GATE_LANES), lambda b, h, i: (rowi(b, h, i), 0)),
            pl.BlockSpec((None, hps, GDN_GATE_SLOTS, tt), lambda b, h, i: (b, h, 0, i)),
            pl.BlockSpec((1, hd), lambda b, h, i: (0, 0)),
        ],
        out_specs=pl.BlockSpec((tt, vw), lambda b, h, i: (rowi(b, h, i), h)),
        out_shape=jax.ShapeDtypeStruct((t, GDN_V_HEADS * hd), BF16),
        scratch_shapes=[pltpu.VMEM((2 * hps, hd, hd), F32)],
        compiler_params=_cparams(3),
    )(p, p, p, p, gcol, grow, norm_w)


def _mlstm_kernel(q_ref, k_ref, v_ref, og_ref, z_ref, mcol_ref, mrow_ref, nw_ref, o_ref, c_ref, *, tt, hps):
    dk, dv = MLSTM_QK_DIM, MLSTM_V_DIM
    q_scale = dk ** -0.5
    n_groups, n_chunks = tt // GROUP, tt // CHUNK

    @pl.when(pl.program_id(2) == 0)
    def _():
        c_ref[...] = jnp.zeros_like(c_ref)

    mcol = pltpu.roll(mcol_ref[...], _lane_slot_shift(pl.program_id(1) * hps, MLSTM_GATE_SLOTS), axis=1)
    _, _, causal, _ = _group_masks()
    lane = lax.broadcasted_iota(jnp.int32, (tt, 128), 1)
    ones_col = jnp.where(lane == 0, 1.0, 0.0).astype(BF16)

    heads = []
    for hh in range(hps):
        q16, k16 = q_ref[:, hh * dk:(hh + 1) * dk], k_ref[:, hh * dk:(hh + 1) * dk]
        v_aug = jnp.concatenate([v_ref[:, hh * dv:(hh + 1) * dv], ones_col], axis=1)
        lane0 = hh * MLSTM_GATE_SLOTS
        b_c, m_start_c = mcol[:, lane0 + 1:lane0 + 2], mcol[:, lane0 + 3:lane0 + 4]
        i_r, b_r, bl_r = mrow_ref[hh, 0:1, :], mrow_ref[hh, 1:2, :], mrow_ref[hh, 2:3, :]
        m_start_r, m_next_r = mrow_ref[hh, 3:4, :], mrow_ref[hh, 4:5, :]
        log_end = bl_r - b_r + i_r
        k_ts =(k16.astype(F32).T * jnp.exp(log_end - m_next_r)).astype(BF16)
        heads.append(dict(q16=q16, k16=k16, v_aug=v_aug, b_c=b_c, b_r=b_r, i_r=i_r, m_inter=b_c + m_start_c,
                          k_ts=k_ts, carry=jnp.exp(bl_r + m_start_r - m_next_r)))

    probs = [(h, g) for h in heads for g in range(n_groups)]
    qks = [_mm_nt(_rows(h["q16"], g), _rows(h["k16"], g)) for h, g in probs]
    log_ds = [jnp.where(causal, _rows(h["b_c"], g) - _lanes(h["b_r"], g) + _lanes(h["i_r"], g), -jnp.inf)
              for h, g in probs]
    row_max = [jnp.max(ld, axis=1, keepdims=True) for ld in log_ds]
    m_ts = [jnp.maximum(_rows(h["m_inter"], g), rm) for (h, g), rm in zip(probs, row_max)]
    weights = [qk * (q_scale * jnp.exp(ld - m_t)) for qk, ld, m_t in zip(qks, log_ds, m_ts)]
    intras = [_mm(s, _rows(h["v_aug"], g)) for (h, g), s in zip(probs, weights)]
    for hi, h in enumerate(heads):
        h["m_t"] = jnp.concatenate(m_ts[hi * n_groups:(hi + 1) * n_groups], axis=0)
        h["intra"] = jnp.concatenate(intras[hi * n_groups:(hi + 1) * n_groups], axis=0)
    for h in heads:
        h["kv"] = [_mm(h["k_ts"][:, c * CHUNK:(c + 1) * CHUNK], h["v_aug"][c * CHUNK:(c + 1) * CHUNK])
                   for c in range(n_chunks)]

    for hh, h in enumerate(heads):
        state = c_ref[hh]
        inters = []
        for c in range(n_chunks):
            inters.append(_mm(h["q16"][c * CHUNK:(c + 1) * CHUNK], state))
            state = state * h["carry"][:, c * CHUNK:c * CHUNK + 1] + h["kv"][c]
        c_ref[hh] = state
        m_t = h["m_t"]
        num = (q_scale * jnp.exp(h["m_inter"] - m_t)) * jnp.concatenate(inters, axis=0) + h["intra"]
        den = jnp.maximum(jnp.abs(num[:, dv:dv + 1]), jnp.exp(-m_t))
        hn = _rms(num[:, 0:dv] / den) * nw_ref[:, hh * dv:(hh + 1) * dv]
        cols = slice(hh * dv, (hh + 1) * dv)
        o_ref[:, cols] = (_sigmoid(og_ref[:, cols].astype(F32)) * hn * _silu(z_ref[:, cols].astype(F32))).astype(BF16)


def _mlstm(p, mcol, mrow, norm_w, batch, seq, tt, hps, col_q, col_k, col_v, col_o, col_z):
    t = batch * seq
    dk, dv = MLSTM_QK_DIM, MLSTM_V_DIM
    qw, vw = hps * dk, hps * dv
    per_seq = seq // tt
    rowi = lambda b, h, i: b * per_seq + i
    return pl.pallas_call(
        functools.partial(_mlstm_kernel, tt=tt, hps=hps),
        grid=(batch, MLSTM_HEADS // hps, per_seq),
        in_specs=[
            pl.BlockSpec((tt, qw), lambda b, h, i: (rowi(b, h, i), col_q // qw + h)),
            pl.BlockSpec((tt, qw), lambda b, h, i: (rowi(b, h, i), col_k // qw + h)),
            pl.BlockSpec((tt, vw), lambda b, h, i: (rowi(b, h, i), col_v // vw + h)),
            pl.BlockSpec((tt, vw), lambda b, h, i: (rowi(b, h, i), col_o // vw + h)),
            pl.BlockSpec((tt, vw), lambda b, h, i: (rowi(b, h, i), col_z // vw + h)),
            pl.BlockSpec((tt, GATE_LANES), lambda b, h, i: (rowi(b, h, i), 0)),
            pl.BlockSpec((None, hps, MLSTM_GATE_SLOTS, tt), lambda b, h, i: (b, h, 0, i)),
            pl.BlockSpec((1, vw), lambda b, h, i: (0, h)),
        ],
        out_specs=pl.BlockSpec((tt, vw), lambda b, h, i: (rowi(b, h, i), h)),
        out_shape=jax.ShapeDtypeStruct((t, MLSTM_HEADS * dv), BF16),
        scratch_shapes=[pltpu.VMEM((hps, dk, dv + 128), F32)],
        compiler_params=_cparams(3),
    )(p, p, p, p, p, mcol, mrow, norm_w)


def _merge_kernel(ya_ref, hb_ref, wa_ref, wb_ref, ga_ref, gb_ref, o_ref):
    ya = jnp.dot(ya_ref[...], wa_ref[...], preferred_element_type=F32)
    yb = jnp.dot(hb_ref[...], wb_ref[...], preferred_element_type=F32)
    merged = _sigmoid(ga_ref[...].astype(F32)) * ya + _sigmoid(gb_ref[...].astype(F32)) * yb
    o_ref[...] = merged.astype(BF16)


def _merge(ya, hb, wa, wb, p, col_ga, col_gb, tm, tn):
    t = ya.shape[0]
    d = wa.shape[1]
    return pl.pallas_call(
        _merge_kernel,
        grid=(t // tm, d // tn),
        in_specs=[pl.BlockSpec((tm, ya.shape[1]), lambda i, j: (i, 0)),
                  pl.BlockSpec((tm, hb.shape[1]), lambda i, j: (i, 0)),
                  pl.BlockSpec((wa.shape[0], tn), lambda i, j: (0, j)),
                  pl.BlockSpec((wb.shape[0], tn), lambda i, j: (0, j)),
                  pl.BlockSpec((tm, tn), lambda i, j: (i, col_ga // tn + j)),
                  pl.BlockSpec((tm, tn), lambda i, j: (i, col_gb // tn + j))],
        out_specs=pl.BlockSpec((tm, tn), lambda i, j: (i, j)),
        out_shape=jax.ShapeDtypeStruct((t, d), BF16),
        compiler_params=_cparams(2),
    )(ya, hb, wa, wb, p, p)


def _out_kernel(mg_ref, w_ref, x_ref, mod_ref, nw_ref, o_ref, *, d):
    out = jnp.dot(mg_ref[...], w_ref[...], preferred_element_type=F32)
    o_ref[...] = x_ref[...] + mod_ref[:, 2 * d:3 * d] * (_rms(out) * nw_ref[...])


def _out(mg, w_out, x2, mod, nw, seq, tm):
    t, d = x2.shape
    per_seq = seq // tm
    return pl.pallas_call(
        functools.partial(_out_kernel, d=d),
        grid=(t // tm,),
        in_specs=[pl.BlockSpec((tm, d), lambda i: (i, 0)),
                  pl.BlockSpec((d, d), lambda i: (0, 0)),
                  pl.BlockSpec((tm, d), lambda i: (i, 0)),
                  pl.BlockSpec((None, 1, 3 * d), lambda i: (i // per_seq, 0, 0)),
                  pl.BlockSpec((1, d), lambda i: (0, 0))],
        out_specs=pl.BlockSpec((tm, d), lambda i: (i, 0)),
        out_shape=jax.ShapeDtypeStruct((t, d), F32),
        compiler_params=_cparams(1),
    )(mg, w_out, x2, mod, nw)


def _gate_columns(d):
    qkv = 2 * GDN_QK_HEADS * GDN_HEAD_DIM + GDN_V_HEADS * GDN_HEAD_DIM
    col_a, col_b = qkv, qkv + GDN_V_HEADS
    col_i = col_b + GDN_V_HEADS + GDN_V_HEADS * GDN_HEAD_DIM + 2 * MLSTM_HEADS * MLSTM_QK_DIM + MLSTM_HEADS * MLSTM_V_DIM
    col_f = col_i + MLSTM_HEADS
    g_src, g_head, g_use = np.zeros(GATE_LANES, np.int32), np.zeros(GATE_LANES, np.int32), np.zeros(GATE_LANES, bool)
    m_src, m_head, m_use = np.zeros(GATE_LANES, np.int32), np.zeros(GATE_LANES, np.int32), np.zeros(GATE_LANES, bool)
    for lane in range(GATE_LANES):
        pair, slot = divmod(lane, GDN_GATE_SLOTS)
        if slot < 6:
            head = 2 * pair + slot % 2
            g_src[lane] = (col_b if slot in (2, 3) else col_a) + head
            g_head[lane], g_use[lane] = head, True
        head, slot = divmod(lane, MLSTM_GATE_SLOTS)
        if slot < 3:
            m_src[lane] = (col_i if slot == 0 else col_f) + head
            m_head[lane], m_use[lane] = head, True
    slot_is_i = (np.arange(GATE_LANES) % MLSTM_GATE_SLOTS) == 0
    return (g_src, g_head, g_use), (m_src, m_head, m_use), slot_is_i


def _layer(x, c, w_ada, b_ada, norm_pre_w, w_in, gdn_conv_w, gdn_A_log, gdn_dt_bias, gdn_norm_w,
           mlstm_conv_w, mlstm_b_i, mlstm_b_f, mlstm_norm_w, w_proj_gdn, w_proj_mlstm, w_out, norm_post_w):
    batch, seq, d = x.shape
    t = batch * seq
    qk_w = GDN_QK_HEADS * GDN_HEAD_DIM
    v_w = GDN_V_HEADS * GDN_HEAD_DIM
    mqk_w = MLSTM_HEADS * MLSTM_QK_DIM
    mv_w = MLSTM_HEADS * MLSTM_V_DIM
    assert d == qk_w == mv_w and seq % 256 == 0

    src_a = 2 * qk_w + v_w
    src_z = src_a + 2 * GDN_V_HEADS
    src_mqk = src_z + v_w
    src_i = src_mqk + 2 * mqk_w + mv_w
    src_o = src_i + 2 * MLSTM_HEADS
    col_q, col_k, col_v, col_z = 0, qk_w, 2 * qk_w, 2 * qk_w + v_w
    col_mq = col_z + v_w
    col_mk, col_mv = col_mq + mqk_w, col_mq + 2 * mqk_w
    col_mo = col_mv + mv_w
    col_mz, col_ga, col_gb = col_mo + mv_w, col_mo + 2 * mv_w, col_mo + 2 * mv_w + d
    n_p = col_gb + d
    tn = 1024
    segments = ((0, 0), (col_z, src_z), (col_mo, src_o))
    assert all(p_col % tn == 0 for p_col, _ in segments) and n_p - col_mo == w_in.shape[1] - src_o
    w_t = jnp.swapaxes(w_in, 0, 1)

    conv_big = jnp.zeros((8, n_p), F32).at[0:CONV_WIDTH, col_q:col_z].set(0.5 * gdn_conv_w)
    conv_big = conv_big.at[0:CONV_WIDTH, col_mq:col_mv].set(0.5 * mlstm_conv_w)
    assert col_z % tn == 0 and col_mq % tn == 0 and col_mv % tn == 0
    conv_tiles = ((col_q // tn, col_z // tn), (col_mq // tn, col_mv // tn))

    (g_src, g_head, g_use), (m_src, m_head, m_use), slot_is_i = _gate_columns(d)
    wg = jnp.where(g_use[None, :], jnp.take(w_t, g_src, axis=0).T, 0.0)
    wm = jnp.where(m_use[None, :], jnp.take(w_t, m_src, axis=0).T, 0.0)
    pg = jnp.stack([jnp.where(g_use, jnp.take(gdn_A_log, g_head), 0.0),
                    jnp.where(g_use, jnp.take(gdn_dt_bias, g_head), 0.0)])
    pm = jnp.where(m_use, jnp.where(slot_is_i, jnp.take(mlstm_b_i, m_head), jnp.take(mlstm_b_f, m_head)), 0.0)[None, :]

    x2 = x.reshape(t, d)
    rows = max(8, batch)
    c8 = jnp.pad(c, ((0, rows - batch), (0, 0)))
    mod = _ada(c8, w_ada, b_ada)[:batch].reshape(batch, 1, 3 * d)
    nw_pre = norm_pre_w.reshape(1, d)

    tm = min(1024, seq)
    p = _inproj(x2, mod, nw_pre, w_t, conv_big, seq, tm=tm, tn=tn, conv_tiles=conv_tiles, segments=segments)
    gcol, grow, mcol, mrow = _gates(x2, mod, nw_pre, jnp.concatenate([wg, wm], axis=1), pg, pm, batch, seq,
                                    tg=min(512, seq))
    grow = grow.reshape(batch, GDN_QK_HEADS, GDN_GATE_SLOTS, seq)
    mrow = mrow.reshape(batch, MLSTM_HEADS, MLSTM_GATE_SLOTS, seq)

    tt = 256
    ya = _gdn(p, gcol, grow, gdn_norm_w.reshape(1, GDN_HEAD_DIM), batch, seq, tt, 8,
              col_q, col_k, col_v, col_z)
    hb = _mlstm(p, mcol, mrow, mlstm_norm_w.reshape(1, mv_w), batch, seq, tt, 8,
                col_mq, col_mk, col_mv, col_mo, col_mz)
    mg = _merge(ya, hb, w_proj_gdn.astype(BF16), w_proj_mlstm.astype(BF16), p, col_ga, col_gb, tm=tm, tn=512)
    y = _out(mg, w_out.astype(BF16), x2, mod, norm_post_w.reshape(1, d), seq, tm=min(512, seq))
    return y.reshape(batch, seq, d)


def kernel(x, c, w_ada, b_ada, norm_pre_w, w_in, gdn_conv_w, gdn_A_log, gdn_dt_bias, gdn_norm_w, mlstm_conv_w, mlstm_b_i, mlstm_b_f, mlstm_norm_w, w_proj_gdn, w_proj_mlstm, w_out, norm_post_w):
    for l in range(w_ada.shape[0]):
        x = _layer(x, c, w_ada[l], b_ada[l], norm_pre_w[l], w_in[l], gdn_conv_w[l], gdn_A_log[l],
                   gdn_dt_bias[l], gdn_norm_w[l], mlstm_conv_w[l], mlstm_b_i[l], mlstm_b_f[l],
                   mlstm_norm_w[l], w_proj_gdn[l], w_proj_mlstm[l], w_out[l], norm_post_w[l])
    return x
```

```python
import functools

import jax
import jax.numpy as jnp
import numpy as np
from jax import lax
from jax.experimental import pallas as pl
from jax.experimental.pallas import tpu as pltpu

F32 = jnp.float32
BF16 = jnp.bfloat16

NORM_EPS = 1e-6
CHUNK = 64
CONV_WIDTH = 4
LANES = 128
F32_SUBLANES = 8
BF16_SUBLANES = 16
CONV_BLOCK = 64
INV_BASE = 8
GROUP = 128
GDN_WAVE = 4

GDN_QK_HEADS = 16
GDN_V_HEADS = 32
GDN_HEAD_DIM = 128
MLSTM_HEADS = 8
MLSTM_QK_DIM = 128
MLSTM_V_DIM = 256
GATE_LANES = 128
GDN_GATE_SLOTS = GATE_LANES // GDN_QK_HEADS
MLSTM_GATE_SLOTS = GATE_LANES // MLSTM_HEADS

VMEM_LIMIT_BYTES = 56 * 1024 * 1024


def _cparams(n_axes):
    return pltpu.CompilerParams(dimension_semantics=("arbitrary",) * n_axes,
                                vmem_limit_bytes=VMEM_LIMIT_BYTES)


def _mm(a, b):
    return jnp.dot(a.astype(BF16), b.astype(BF16), preferred_element_type=F32)


def _mm_nt(a, b):
    return lax.dot_general(a.astype(BF16), b.astype(BF16), (((1,), (1,)), ((), ())),
                           preferred_element_type=F32)


def _sigmoid(x):
    return 0.5 + 0.5 * jnp.tanh(0.5 * x)


def _silu(x):
    half = 0.5 * x
    return half + half * jnp.tanh(half)


def _softplus(x):
    return jnp.maximum(x, 0.0) + jnp.log(1.0 + jnp.exp(-jnp.abs(x)))


def _rms(x):
    return x * lax.rsqrt(jnp.mean(x * x, axis=-1, keepdims=True) + NORM_EPS)


def _prenorm(x, mod_ref, nw_ref, d):
    y = _rms(x) * nw_ref[...]
    return y * (1.0 + mod_ref[:, d:2 * d]) + mod_ref[:, 0:d]


def _ada_kernel(c_ref, w_ref, b_ref, o_ref):
    o_ref[...] = _mm(_silu(c_ref[...]), w_ref[...]) + b_ref[...]


def _ada(c8, w_ada, b_ada, tn=512):
    rows, d = c8.shape
    n = w_ada.shape[1]
    return pl.pallas_call(
        _ada_kernel,
        grid=(n // tn,),
        in_specs=[pl.BlockSpec((rows, d), lambda j: (0, 0)),
                  pl.BlockSpec((d, tn), lambda j: (0, j)),
                  pl.BlockSpec((1, tn), lambda j: (0, j))],
        out_specs=pl.BlockSpec((rows, tn), lambda j: (0, j)),
        out_shape=jax.ShapeDtypeStruct((rows, n), F32),
        compiler_params=_cparams(1),
    )(c8, w_ada, b_ada.reshape(1, n))


def _inproj_kernel(x_ref, mod_ref, nw_ref, w_ref, cw_ref, o_ref, h_ref, hperm_ref, tail_ref, unperm_ref, stage_ref,
                   *, d, tm, sub, per_seq, conv_tiles, l2_tiles):
    i, j = pl.program_id(0), pl.program_id(1)
    tn = o_ref.shape[1]
    slab = F32_SUBLANES
    n_slabs = CONV_BLOCK // slab

    @pl.when(j == 0)
    def _():
        def block(blk, carry):
            base = pl.multiple_of(blk * CONV_BLOCK, CONV_BLOCK)
            hb = _prenorm(x_ref[pl.ds(base, CONV_BLOCK), :], mod_ref, nw_ref, d)
            h_ref[pl.ds(base, CONV_BLOCK), :] = hb.astype(BF16)
            for lt in range(d // LANES):
                stage_ref[lt] = hb[:, lt * LANES:(lt + 1) * LANES]
            for s in range(0, n_slabs, 2):
                rows = jnp.concatenate(
                    [jnp.concatenate([stage_ref[lt, pl.ds(s + e, slab, stride=n_slabs), :]
                                      for lt in range(d // LANES)], axis=1) for e in range(2)], axis=0)
                hperm_ref[pl.ds(base + slab * s, 2 * slab), :] = rows.astype(BF16)
            return carry

        lax.fori_loop(0, tm // CONV_BLOCK, block, 0)

    is_conv = functools.reduce(jnp.logical_or, [(j >= lo) & (j < hi) for lo, hi in conv_tiles])

    def project(lhs_ref, r):
        return lax.dot_general(lhs_ref[r * sub:(r + 1) * sub, :], w_ref[...].astype(BF16),
                               (((1,), (1,)), ((), ())), preferred_element_type=F32)

    @pl.when(jnp.logical_not(is_conv))
    def _():
        for r in range(tm // sub):
            o_ref[r * sub:(r + 1) * sub, :] = project(h_ref, r).astype(BF16)

    is_l2 = j < l2_tiles
    l2_scale = jnp.where(j < l2_tiles // 2, LANES ** -0.5, 1.0)

    def conv_tile(l2):
        @pl.when((i % per_seq) == 0)
        def _():
            tail_ref[j] = jnp.zeros(tail_ref.shape[1:], F32)

        cw = cw_ref[...]
        top = lax.broadcasted_iota(jnp.int32, (slab, tn), 0) == 0
        keep = n_slabs - (CONV_WIDTH - 1)
        prev = [pltpu.roll(tail_ref[j, k * slab:(k + 1) * slab, :], 1, axis=0) for k in range(CONV_WIDTH - 1)]
        last = None
        for r in range(tm // sub):
            acc = project(hperm_ref, r)
            for blk in range(sub // CONV_BLOCK):
                row0 = blk * CONV_BLOCK
                slabs = [acc[row0 + s * slab:row0 + (s + 1) * slab, :] for s in range(n_slabs)]
                moved = [pltpu.roll(slabs[keep + k], 1, axis=0) for k in range(CONV_WIDTH - 1)]
                wrapped = [jnp.where(top, prev[k], moved[k]) for k in range(CONV_WIDTH - 1)]
                prev, last = moved, slabs
                for s in range(n_slabs):
                    half = cw[CONV_WIDTH - 1:CONV_WIDTH, :] * slabs[s]
                    for back in range(1, CONV_WIDTH):
                        src = slabs[s - back] if s >= back else wrapped[s - back + n_slabs - keep]
                        half = half + cw[CONV_WIDTH - 1 - back:CONV_WIDTH - back, :] * src
                    out = half + half * jnp.tanh(half)
                    for lt in range(tn // LANES):
                        head = out[:, lt * LANES:(lt + 1) * LANES]
                        if l2:
                            head = head * (lax.rsqrt(jnp.sum(head * head, axis=-1, keepdims=True) + NORM_EPS) * l2_scale)
                        unperm_ref[lt, pl.ds(row0 + s, slab, stride=n_slabs), :] = head
            for lt in range(tn // LANES):
                o_ref[r * sub:(r + 1) * sub, lt * LANES:(lt + 1) * LANES] = unperm_ref[lt].astype(BF16)
        tail_ref[j] = jnp.concatenate(last[keep:], axis=0)

    @pl.when(is_l2)
    def _():
        conv_tile(True)

    @pl.when(is_conv & jnp.logical_not(is_l2))
    def _():
        conv_tile(False)


def _inproj(x2, mod, nw, w_t, conv_big, seq, tm, tn, conv_tiles, l2_tiles, segments):
    t, d = x2.shape
    n = conv_big.shape[1]
    per_seq = seq // tm
    sub = min(256, tm)

    def w_row(j):
        units, prev = j * (tn // BF16_SUBLANES), 0
        for p_col, src_col in segments[1:]:
            delta = src_col - p_col
            assert p_col % tn == 0 and (delta - prev) % BF16_SUBLANES == 0
            units = units + (j >= p_col // tn).astype(jnp.int32) * ((delta - prev) // BF16_SUBLANES)
            prev = delta
        return units * BF16_SUBLANES

    return pl.pallas_call(
        functools.partial(_inproj_kernel, d=d, tm=tm, sub=sub, per_seq=per_seq, conv_tiles=conv_tiles,
                          l2_tiles=l2_tiles),
        grid=(t // tm, n // tn),
        in_specs=[pl.BlockSpec((tm, d), lambda i, j: (i, 0)),
                  pl.BlockSpec((None, 1, 3 * d), lambda i, j: (i // per_seq, 0, 0)),
                  pl.BlockSpec((1, d), lambda i, j: (0, 0)),
                  pl.BlockSpec((pl.Element(tn), pl.Element(d)), lambda i, j: (w_row(j), 0)),
                  pl.BlockSpec((8, tn), lambda i, j: (0, j))],
        out_specs=pl.BlockSpec((tm, tn), lambda i, j: (i, j)),
        out_shape=jax.ShapeDtypeStruct((t, n), BF16),
        scratch_shapes=[pltpu.VMEM((tm, d), BF16), pltpu.VMEM((tm, d), BF16),
                        pltpu.VMEM((n // tn, (CONV_WIDTH - 1) * F32_SUBLANES, tn), F32),
                        pltpu.VMEM((tn // LANES, sub, LANES), F32), pltpu.VMEM((d // LANES, CONV_BLOCK, LANES), F32)],
        compiler_params=_cparams(2),
    )(x2, mod, nw, w_t, conv_big)


def _same_block(ri, ci, size):
    shift = size.bit_length() - 1
    return (ri >> shift) == (ci >> shift)


def _gates_kernel(x_ref, mod_ref, nw_ref, whi_ref, wlo_ref, pg_ref, pm_ref,
                  gcol_ref, grow_ref, mcol_ref, mrow_ref, mstate_ref, *, d, tg, per_seq):
    h = _prenorm(x_ref[...], mod_ref, nw_ref, d)
    h_hi = h.astype(BF16)
    h_lo = (h - h_hi.astype(F32)).astype(BF16)
    w_hi, w_lo = whi_ref[...], wlo_ref[...]
    y = (jnp.dot(h_hi, w_hi, preferred_element_type=F32)
         + (jnp.dot(h_hi, w_lo, preferred_element_type=F32) + jnp.dot(h_lo, w_hi, preferred_element_type=F32)))
    yg, ym = y[:, :GATE_LANES], y[:, GATE_LANES:]
    lane = lax.broadcasted_iota(jnp.int32, (tg, GATE_LANES), 1)

    slot = lane & (GDN_GATE_SLOTS - 1)
    g = -jnp.exp(pg_ref[0:1, :]) * _softplus(yg + pg_ref[1:2, :])
    is_g = (slot == 0) | (slot == 1) | (slot == 4) | (slot == 5)
    is_beta = (slot == 2) | (slot == 3)
    y_g = jnp.where(is_g, g, jnp.where(is_beta, _sigmoid(yg), 0.0))

    mslot = lane & (MLSTM_GATE_SLOTS - 1)
    pre = ym + pm_ref[0:1, :]
    y_m = jnp.where(mslot == 0, pre, jnp.where((mslot == 1) | (mslot == 2), -_softplus(-pre), 0.0))

    ri = lax.broadcasted_iota(jnp.int32, (tg, tg), 0)
    ci = lax.broadcasted_iota(jnp.int32, (tg, tg), 1)
    same = _same_block(ri, ci, CHUNK)
    lower = jnp.where(same & (ri >= ci), 1.0, 0.0).astype(BF16)
    total = jnp.where(same, 1.0, 0.0).astype(BF16)
    y_all = jnp.concatenate([y_g, y_m], axis=1)
    parts = []
    rest = y_all
    for _ in range(3):
        piece = rest.astype(BF16)
        parts.append(piece)
        rest = rest - piece.astype(F32)
    cum = sum(jnp.dot(lower, piece, preferred_element_type=F32) for piece in reversed(parts))
    tot = sum(jnp.dot(total, piece, preferred_element_type=F32) for piece in reversed(parts))

    g_out = jnp.where(slot < 2, cum[:, :GATE_LANES],
                      jnp.where((slot == 4) | (slot == 5), tot[:, :GATE_LANES], y_g))
    m_out = jnp.where(mslot == 1, cum[:, GATE_LANES:],
                      jnp.where(mslot == 2, tot[:, GATE_LANES:], y_m))

    @pl.when(pl.program_id(0) % per_seq == 0)
    def _():
        mstate_ref[...] = jnp.zeros_like(mstate_ref)

    b_al = pltpu.roll(m_out, GATE_LANES - 1, axis=1)
    bl_al = pltpu.roll(m_out, GATE_LANES - 2, axis=1)
    log_end = bl_al - b_al + m_out
    m = mstate_ref[0:1, :]
    m_starts, m_nexts = [], []
    for c in range(tg // CHUNK):
        r0 = c * CHUNK
        m_starts.append(jnp.broadcast_to(m, (CHUNK, GATE_LANES)))
        m = jnp.maximum(bl_al[r0:r0 + 1, :] + m, jnp.max(log_end[r0:r0 + CHUNK, :], axis=0, keepdims=True))
        m_nexts.append(jnp.broadcast_to(m, (CHUNK, GATE_LANES)))
    mstate_ref[...] = jnp.broadcast_to(m, mstate_ref.shape)
    m_out = jnp.where(mslot == 3, pltpu.roll(jnp.concatenate(m_starts, axis=0), 3, axis=1),
                      jnp.where(mslot == 4, pltpu.roll(jnp.concatenate(m_nexts, axis=0), 4, axis=1), m_out))
    gcol_ref[...] = g_out
    grow_ref[...] = g_out.T
    mcol_ref[...] = m_out
    mrow_ref[...] = m_out.T


def _gates(x2, mod, nw, w_gates, pg, pm, batch, seq, tg):
    t, d = x2.shape
    per_seq = seq // tg
    w_hi = w_gates.astype(BF16)
    w_lo = (w_gates - w_hi.astype(F32)).astype(BF16)
    col = jax.ShapeDtypeStruct((t, GATE_LANES), F32)
    row = jax.ShapeDtypeStruct((batch, GATE_LANES, seq), F32)
    col_spec = pl.BlockSpec((tg, GATE_LANES), lambda i: (i, 0))
    row_spec = pl.BlockSpec((None, GATE_LANES, tg), lambda i: (i // per_seq, 0, i % per_seq))
    const = lambda shape: pl.BlockSpec(shape, lambda i: (0, 0))
    return pl.pallas_call(
        functools.partial(_gates_kernel, d=d, tg=tg, per_seq=per_seq),
        grid=(t // tg,),
        in_specs=[pl.BlockSpec((tg, d), lambda i: (i, 0)),
                  pl.BlockSpec((None, 1, 3 * d), lambda i: (i // per_seq, 0, 0)),
                  const((1, d)), const((d, 2 * GATE_LANES)), const((d, 2 * GATE_LANES)),
                  const((2, GATE_LANES)), const((1, GATE_LANES))],
        out_specs=[col_spec, row_spec, col_spec, row_spec],
        out_shape=[col, row, col, row],
        scratch_shapes=[pltpu.VMEM((8, GATE_LANES), F32)],
        compiler_params=_cparams(1),
    )(x2, mod, nw, w_hi, w_lo, pg, pm)


def _unit_lower_inverses(mats, ri, ci):
    diag = _same_block(ri, ci, INV_BASE)
    eye = jnp.where(ri == ci, 1.0, 0.0)
    pows = [jnp.where(diag, a, 0.0) for a in mats]
    invs = [eye - ab for ab in pows]
    order = 2
    while order < INV_BASE:
        pows = [_mm(ab, ab) for ab in pows]
        yield
        invs = [inv + _mm(inv, ab) for inv, ab in zip(invs, pows)]
        yield
        order *= 2
    size = INV_BASE
    while size < CHUNK:
        merged = _same_block(ri, ci, 2 * size)
        keep = merged & jnp.logical_not(diag)
        corr = [_mm(_odd_blocks(jnp.where(keep, a, 0.0), size), inv) for a, inv in zip(mats, invs)]
        yield
        zeros = jnp.zeros_like(invs[0])
        invs = [_with_odd_blocks(inv, _odd_blocks(inv, size) - _mm(_odd_blocks(inv, size),
                                                                     _with_odd_blocks(zeros, cr, size)), size)
                for inv, cr in zip(invs, corr)]
        yield
        diag, size = merged, 2 * size
    return invs


def _odd_blocks(x, size):
    return jnp.concatenate([x[b * size:(b + 1) * size] for b in range(1, x.shape[0] // size, 2)], axis=0)


def _with_odd_blocks(x, odd, size):
    blocks = [odd[(b // 2) * size:(b // 2 + 1) * size] if b % 2 else x[b * size:(b + 1) * size]
              for b in range(x.shape[0] // size)]
    return jnp.concatenate(blocks, axis=0)


def _interleave(*gens):
    live = list(gens)
    while live:
        for gen in list(live):
            try:
                next(gen)
            except StopIteration:
                live.remove(gen)


def _lane_slot_shift(group, slots):
    return lax.rem((GATE_LANES // slots - group) * slots, GATE_LANES)


def _group_masks():
    ri = lax.broadcasted_iota(jnp.int32, (GROUP, GROUP), 0)
    ci = lax.broadcasted_iota(jnp.int32, (GROUP, GROUP), 1)
    same = _same_block(ri, ci, CHUNK)
    return ri, ci, same & (ri >= ci), same & (ri > ci)


def _rows(x, g):
    return x[g * GROUP:(g + 1) * GROUP]


def _lanes(x, g):
    return x[:, g * GROUP:(g + 1) * GROUP]


def _gdn_kernel(q_ref, k_ref, v_ref, z_ref, gcol_ref, grow_ref, nw_ref, o_ref, s_ref, *, tt, hps):
    hd = GDN_HEAD_DIM
    n_groups, n_chunks, per_group = tt // GROUP, tt // CHUNK, GROUP // CHUNK

    @pl.when(pl.program_id(2) == 0)
    def _():
        s_ref[...] = jnp.zeros_like(s_ref)

    gcol = pltpu.roll(gcol_ref[...], _lane_slot_shift(pl.program_id(1) * hps, GDN_GATE_SLOTS), axis=1)
    ri, ci, causal, strict = _group_masks()
    n_waves = hps // GDN_WAVE
    waves = [dict(pairs=range(w * GDN_WAVE, (w + 1) * GDN_WAVE), heads=[], a_mats=[], done=[]) for w in range(n_waves)]

    def prep(w):
        return _gdn_prep(waves[w], q_ref, k_ref, v_ref, gcol, grow_ref, causal, strict, n_groups)

    def solve(w):
        return _gdn_solve(waves[w], s_ref, ri, ci, n_groups, n_chunks, per_group)

    def finish(w):
        for vh, state, out in waves[w]["done"]:
            s_ref[vh] = state
            o = _rms(out) * nw_ref[...]
            z = z_ref[:, vh * hd:(vh + 1) * hd].astype(F32)
            o_ref[:, vh * hd:(vh + 1) * hd] = (o * _silu(z)).astype(BF16)
            yield

    _interleave(prep(0))
    for w in range(n_waves):
        _interleave(solve(w), *([prep(w + 1)] if w + 1 < n_waves else []), *([finish(w - 1)] if w > 0 else []))
    _interleave(finish(n_waves - 1))


def _gdn_prep(wave, q_ref, k_ref, v_ref, gcol, grow_ref, causal, strict, n_groups):
    hd = GDN_HEAD_DIM
    heads, a_mats = wave["heads"], wave["a_mats"]
    for pp in wave["pairs"]:
        q16, k16 = q_ref[:, pp * hd:(pp + 1) * hd], k_ref[:, pp * hd:(pp + 1) * hd]
        q, k_t = q16.astype(F32), k16.astype(F32).T
        kk = [_mm_nt(_rows(k16, g), _rows(k16, g)) for g in range(n_groups)]
        qk = [_mm_nt(_rows(q16, g), _rows(k16, g)) for g in range(n_groups)]
        yield
        for hh in range(2):
            lane = pp * GDN_GATE_SLOTS + hh
            gc_c, be_c = gcol[:, lane:lane + 1], gcol[:, lane + 2:lane + 3]
            gc_r, be_r, gl_r = grow_ref[pp, hh:hh + 1, :], grow_ref[pp, 2 + hh:3 + hh, :], grow_ref[pp, 4 + hh:5 + hh, :]
            decay = [jnp.exp(jnp.where(causal, _rows(gc_c, g) - _lanes(gc_r, g), -jnp.inf)) for g in range(n_groups)]
            a_mats += [jnp.where(strict, _rows(be_c, g) * kk[g] * decay[g], 0.0) for g in range(n_groups)]
            vh = 2 * pp + hh
            heads.append(dict(vh=vh, k16=k16, qk=qk, decay=decay, t_scale=be_r, w_scale=be_r * jnp.exp(gc_r),
                              qd=q * jnp.exp(gc_c), kt_t=(k_t * jnp.exp(gl_r - gc_r)).astype(BF16),
                              g_tot=jnp.exp(gl_r), v=v_ref[:, vh * hd:(vh + 1) * hd]))
            yield


def _gdn_solve(wave, s_ref, ri, ci, n_groups, n_chunks, per_group):
    hd = GDN_HEAD_DIM
    heads = wave["heads"]
    invs = yield from _unit_lower_inverses(wave["a_mats"], ri, ci)

    for hi, h in enumerate(heads):
        h["wu"] = []
        for g in range(n_groups):
            inv = invs[hi * n_groups + g]
            w = _mm(inv * _lanes(h["w_scale"], g), _rows(h["k16"], g))
            u = _mm(inv * _lanes(h["t_scale"], g), _rows(h["v"], g))
            h["wu"].append(jnp.concatenate([w, u], axis=1).astype(BF16))
        yield
    for h in heads:
        h["awu"] = [_mm(h["qk"][g] * h["decay"][g], h["wu"][g]) for g in range(n_groups)]
    yield
    for h in heads:
        h["ktwu"] = []
        for c in range(n_chunks):
            g, r0 = c // per_group, (c % per_group) * CHUNK
            h["ktwu"].append(_mm(h["kt_t"][:, c * CHUNK:(c + 1) * CHUNK], h["wu"][g][r0:r0 + CHUNK]))
        yield

    states = [s_ref[h["vh"]] for h in heads]
    outs = [[] for _ in heads]
    for c in range(n_chunks):
        g, r0 = c // per_group, (c % per_group) * CHUNK
        for hi, h in enumerate(heads):
            aw, au = h["awu"][g][r0:r0 + CHUNK, 0:hd], h["awu"][g][r0:r0 + CHUNK, hd:2 * hd]
            q_eff = h["qd"][c * CHUNK:(c + 1) * CHUNK] - aw
            prod = _mm(jnp.concatenate([h["ktwu"][c][:, 0:hd], q_eff], axis=0), states[hi])
            outs[hi].append(prod[hd:hd + CHUNK] + au)
            states[hi] = (states[hi] * h["g_tot"][:, c * CHUNK:c * CHUNK + 1] - prod[0:hd]
                          + h["ktwu"][c][:, hd:2 * hd])
        yield
    wave["done"].extend((h["vh"], states[hi], jnp.concatenate(outs[hi], axis=0)) for hi, h in enumerate(heads))


def _gdn(p,gcol, grow, norm_w, batch, seq, tt, hps, col_q, col_k, col_v, col_z):
    t = batch * seq
    hd = GDN_HEAD_DIM
    qw, vw = hps * hd, 2 * hps * hd
    per_seq = seq // tt
    rowi = lambda b, h, i: b * per_seq + i
    return pl.pallas_call(
        functools.partial(_gdn_kernel, tt=tt, hps=hps),
        grid=(batch, GDN_QK_HEADS // hps, per_seq),
        in_specs=[
            pl.BlockSpec((tt, qw), lambda b, h, i: (rowi(b, h, i), col_q // qw + h)),
            pl.BlockSpec((tt, qw), lambda b, h, i: (rowi(b, h, i), col_k // qw + h)),
            pl.BlockSpec((tt, vw), lambda b, h, i: (rowi(b, h, i), col_v // vw + h)),
            pl.BlockSpec((tt, vw), lambda b, h, i: (rowi(b, h, i), col_z // vw + h)),
            pl.BlockSpec((tt, GATE_LANES), lambda b, h, i: (rowi(b, h, i), 0)),
            pl.BlockSpec((None, hps, GDN_GATE_SLOTS, tt), lambda b, h, i: (b, h, 0, i)),
            pl.BlockSpec((1, hd), lambda b, h, i: (0, 0)),
        ],
        out_specs=pl.BlockSpec((tt, vw), lambda b, h, i: (rowi(b, h, i), h)),
        out_shape=jax.ShapeDtypeStruct((t, GDN_V_HEADS * hd), BF16),
        scratch_shapes=[pltpu.VMEM((2 * hps, hd, hd), F32)],
        compiler_params=_cparams(3),
    )(p, p, p, p, gcol, grow, norm_w)


def _mlstm_kernel(q_ref, k_ref, v_ref, og_ref, z_ref, mcol_ref, mrow_ref, nw_ref, o_ref, c_ref, *, tt, hps):
    dk, dv = MLSTM_QK_DIM, MLSTM_V_DIM
    q_scale = dk ** -0.5
    n_groups, n_chunks = tt // GROUP, tt // CHUNK

    @pl.when(pl.program_id(2) == 0)
    def _():
        c_ref[...] = jnp.zeros_like(c_ref)

    mcol = pltpu.roll(mcol_ref[...], _lane_slot_shift(pl.program_id(1) * hps, MLSTM_GATE_SLOTS), axis=1)
    _, _, causal, _ = _group_masks()
    lane = lax.broadcasted_iota(jnp.int32, (tt, 128), 1)
    ones_col = jnp.where(lane == 0, 1.0, 0.0).astype(BF16)

    heads = []
    for hh in range(hps):
        q16, k16 = q_ref[:, hh * dk:(hh + 1) * dk], k_ref[:, hh * dk:(hh + 1) * dk]
        v_aug = jnp.concatenate([v_ref[:, hh * dv:(hh + 1) * dv], ones_col], axis=1)
        lane0 = hh * MLSTM_GATE_SLOTS
        b_c, m_start_c = mcol[:, lane0 + 1:lane0 + 2], mcol[:, lane0 + 3:lane0 + 4]
        i_r, b_r, bl_r = mrow_ref[hh, 0:1, :], mrow_ref[hh, 1:2, :], mrow_ref[hh, 2:3, :]
        m_start_r, m_next_r = mrow_ref[hh, 3:4, :], mrow_ref[hh, 4:5, :]
        log_end = bl_r - b_r + i_r
        k_ts =(k16.astype(F32).T * jnp.exp(log_end - m_next_r)).astype(BF16)
        heads.append(dict(q16=q16, k16=k16, v_aug=v_aug, b_c=b_c, b_r=b_r, i_r=i_r, m_inter=b_c + m_start_c,
                          k_ts=k_ts, carry=jnp.exp(bl_r + m_start_r - m_next_r)))

    probs = [(h, g) for h in heads for g in range(n_groups)]
    qks = [_mm_nt(_rows(h["q16"], g), _rows(h["k16"], g)) for h, g in probs]
    log_ds = [jnp.where(causal, _rows(h["b_c"], g) - _lanes(h["b_r"], g) + _lanes(h["i_r"], g), -jnp.inf)
              for h, g in probs]
    row_max = [jnp.max(ld, axis=1, keepdims=True) for ld in log_ds]
    m_ts = [jnp.maximum(_rows(h["m_inter"], g), rm) for (h, g), rm in zip(probs, row_max)]
    weights = [qk * (q_scale * jnp.exp(ld - m_t)) for qk, ld, m_t in zip(qks, log_ds, m_ts)]
    intras = [_mm(s, _rows(h["v_aug"], g)) for (h, g), s in zip(probs, weights)]
    for hi, h in enumerate(heads):
        h["m_t"] = jnp.concatenate(m_ts[hi * n_groups:(hi + 1) * n_groups], axis=0)
        h["intra"] = jnp.concatenate(intras[hi * n_groups:(hi + 1) * n_groups], axis=0)
    for h in heads:
        h["kv"] = [_mm(h["k_ts"][:, c * CHUNK:(c + 1) * CHUNK], h["v_aug"][c * CHUNK:(c + 1) * CHUNK])
                   for c in range(n_chunks)]

    for hh, h in enumerate(heads):
        state = c_ref[hh]
        inters = []
        for c in range(n_chunks):
            inters.append(_mm(h["q16"][c * CHUNK:(c + 1) * CHUNK], state))
            state = state * h["carry"][:, c * CHUNK:c * CHUNK + 1] + h["kv"][c]
        c_ref[hh] = state
        m_t = h["m_t"]
        num = (q_scale * jnp.exp(h["m_inter"] - m_t)) * jnp.concatenate(inters, axis=0) + h["intra"]
        den = jnp.maximum(jnp.abs(num[:, dv:dv + 1]), jnp.exp(-m_t))
        hn = _rms(num[:, 0:dv] / den) * nw_ref[:, hh * dv:(hh + 1) * dv]
        cols = slice(hh * dv, (hh + 1) * dv)
        o_ref[:, cols] = (_sigmoid(og_ref[:, cols].astype(F32)) * hn * _silu(z_ref[:, cols].astype(F32))).astype(BF16)


def _mlstm(p, mcol, mrow, norm_w, batch, seq, tt, hps, col_q, col_k, col_v, col_o, col_z):
    t = batch * seq
    dk, dv = MLSTM_QK_DIM, MLSTM_V_DIM
    qw, vw = hps * dk, hps * dv
    per_seq = seq // tt
    rowi = lambda b, h, i: b * per_seq + i
    return pl.pallas_call(
        functools.partial(_mlstm_kernel, tt=tt, hps=hps),
        grid=(batch, MLSTM_HEADS // hps, per_seq),
        in_specs=[
            pl.BlockSpec((tt, qw), lambda b, h, i: (rowi(b, h, i), col_q // qw + h)),
            pl.BlockSpec((tt, qw), lambda b, h, i: (rowi(b, h, i), col_k // qw + h)),
            pl.BlockSpec((tt, vw), lambda b, h, i: (rowi(b, h, i), col_v // vw + h)),
            pl.BlockSpec((tt, vw), lambda b, h, i: (rowi(b, h, i), col_o // vw + h)),
            pl.BlockSpec((tt, vw), lambda b, h, i: (rowi(b, h, i), col_z // vw + h)),
            pl.BlockSpec((tt, GATE_LANES), lambda b, h, i: (rowi(b, h, i), 0)),
            pl.BlockSpec((None, hps, MLSTM_GATE_SLOTS, tt), lambda b, h, i: (b, h, 0, i)),
            pl.BlockSpec((1, vw), lambda b, h, i: (0, h)),
        ],
        out_specs=pl.BlockSpec((tt, vw), lambda b, h, i: (rowi(b, h, i), h)),
        out_shape=jax.ShapeDtypeStruct((t, MLSTM_HEADS * dv), BF16),
        scratch_shapes=[pltpu.VMEM((hps, dk, dv + 128), F32)],
        compiler_params=_cparams(3),
    )(p, p, p, p, p, mcol, mrow, norm_w)


def _merge_kernel(ya_ref, hb_ref, wa_ref, wb_ref, ga_ref, gb_ref, o_ref):
    ya = jnp.dot(ya_ref[...], wa_ref[...], preferred_element_type=F32)
    yb = jnp.dot(hb_ref[...], wb_ref[...], preferred_element_type=F32)
    merged = _sigmoid(ga_ref[...].astype(F32)) * ya + _sigmoid(gb_ref[...].astype(F32)) * yb
    o_ref[...] = merged.astype(BF16)


def _merge(ya, hb, wa, wb, p, col_ga, col_gb, tm, tn):
    t = ya.shape[0]
    d = wa.shape[1]
    return pl.pallas_call(
        _merge_kernel,
        grid=(t // tm, d // tn),
        in_specs=[pl.BlockSpec((tm, ya.shape[1]), lambda i, j: (i, 0)),
                  pl.BlockSpec((tm, hb.shape[1]), lambda i, j: (i, 0)),
                  pl.BlockSpec((wa.shape[0], tn), lambda i, j: (0, j)),
                  pl.BlockSpec((wb.shape[0], tn), lambda i, j: (0, j)),
                  pl.BlockSpec((tm, tn), lambda i, j: (i, col_ga // tn + j)),
                  pl.BlockSpec((tm, tn), lambda i, j: (i, col_gb // tn + j))],
        out_specs=pl.BlockSpec((tm, tn), lambda i, j: (i, j)),
        out_shape=jax.ShapeDtypeStruct((t, d), BF16),
        compiler_params=_cparams(2),
    )(ya, hb, wa, wb, p, p)


def _out_kernel(mg_ref, w_ref, x_ref, mod_ref, nw_ref, o_ref, *, d):
    out = jnp.dot(mg_ref[...], w_ref[...], preferred_element_type=F32)
    o_ref[...] = x_ref[...] + mod_ref[:, 2 * d:3 * d] * (_rms(out) * nw_ref[...])


def _out(mg, w_out, x2, mod, nw, seq, tm):
    t, d = x2.shape
    per_seq = seq // tm
    return pl.pallas_call(
        functools.partial(_out_kernel, d=d),
        grid=(t // tm,),
        in_specs=[pl.BlockSpec((tm, d), lambda i: (i, 0)),
                  pl.BlockSpec((d, d), lambda i: (0, 0)),
                  pl.BlockSpec((tm, d), lambda i: (i, 0)),
                  pl.BlockSpec((None, 1, 3 * d), lambda i: (i // per_seq, 0, 0)),
                  pl.BlockSpec((1, d), lambda i: (0, 0))],
        out_specs=pl.BlockSpec((tm, d), lambda i: (i, 0)),
        out_shape=jax.ShapeDtypeStruct((t, d), F32),
        compiler_params=_cparams(1),
    )(mg, w_out, x2, mod, nw)


def _gate_columns(d):
    qkv = 2 * GDN_QK_HEADS * GDN_HEAD_DIM + GDN_V_HEADS * GDN_HEAD_DIM
    col_a, col_b = qkv, qkv + GDN_V_HEADS
    col_i = col_b + GDN_V_HEADS + GDN_V_HEADS * GDN_HEAD_DIM + 2 * MLSTM_HEADS * MLSTM_QK_DIM + MLSTM_HEADS * MLSTM_V_DIM
    col_f = col_i + MLSTM_HEADS
    g_src, g_head, g_use = np.zeros(GATE_LANES, np.int32), np.zeros(GATE_LANES, np.int32), np.zeros(GATE_LANES, bool)
    m_src, m_head, m_use = np.zeros(GATE_LANES, np.int32), np.zeros(GATE_LANES, np.int32), np.zeros(GATE_LANES, bool)
    for lane in range(GATE_LANES):
        pair, slot = divmod(lane, GDN_GATE_SLOTS)
        if slot < 6:
            head = 2 * pair + slot % 2
            g_src[lane] = (col_b if slot in (2, 3) else col_a) + head
            g_head[lane], g_use[lane] = head, True
        head, slot = divmod(lane, MLSTM_GATE_SLOTS)
        if slot < 3:
            m_src[lane] = (col_i if slot == 0 else col_f) + head
            m_head[lane], m_use[lane] = head, True
    slot_is_i = (np.arange(GATE_LANES) % MLSTM_GATE_SLOTS) == 0
    return (g_src, g_head, g_use), (m_src, m_head, m_use), slot_is_i


def _layer(x, c, w_ada, b_ada, norm_pre_w, w_in, gdn_conv_w, gdn_A_log, gdn_dt_bias, gdn_norm_w,
           mlstm_conv_w, mlstm_b_i, mlstm_b_f, mlstm_norm_w, w_proj_gdn, w_proj_mlstm, w_out, norm_post_w):
    batch, seq, d = x.shape
    t = batch * seq
    qk_w = GDN_QK_HEADS * GDN_HEAD_DIM
    v_w = GDN_V_HEADS * GDN_HEAD_DIM
    mqk_w = MLSTM_HEADS * MLSTM_QK_DIM
    mv_w = MLSTM_HEADS * MLSTM_V_DIM
    assert d == qk_w == mv_w and seq % 256 == 0

    src_a = 2 * qk_w + v_w
    src_z = src_a + 2 * GDN_V_HEADS
    src_mqk = src_z + v_w
    src_i = src_mqk + 2 * mqk_w + mv_w
    src_o = src_i + 2 * MLSTM_HEADS
    col_q, col_k, col_v, col_z = 0, qk_w, 2 * qk_w, 2 * qk_w + v_w
    col_mq = col_z + v_w
    col_mk, col_mv = col_mq + mqk_w, col_mq + 2 * mqk_w
    col_mo = col_mv + mv_w
    col_mz, col_ga, col_gb = col_mo + mv_w, col_mo + 2 * mv_w, col_mo + 2 * mv_w + d
    n_p = col_gb + d
    tn = 1024
    segments = ((0, 0), (col_z, src_z), (col_mo, src_o))
    assert all(p_col % tn == 0 for p_col, _ in segments) and n_p - col_mo == w_in.shape[1] - src_o
    w_t = jnp.swapaxes(w_in, 0, 1)

    conv_big = jnp.zeros((8, n_p), F32).at[0:CONV_WIDTH, col_q:col_z].set(0.5 * gdn_conv_w)
    conv_big = conv_big.at[0:CONV_WIDTH, col_mq:col_mv].set(0.5 * mlstm_conv_w)
    assert col_z % tn == 0 and col_mq % tn == 0 and col_mv % tn == 0
    conv_tiles = ((col_q // tn, col_z // tn), (col_mq // tn, col_mv // tn))
    assert col_q == 0 and col_k % tn == 0 and col_v == 2 * col_k and GDN_HEAD_DIM == LANES
    l2_tiles = col_v // tn

    (g_src, g_head, g_use), (m_src, m_head, m_use), slot_is_i = _gate_columns(d)
    wg = jnp.where(g_use[None, :], jnp.take(w_t, g_src, axis=0).T, 0.0)
    wm = jnp.where(m_use[None, :], jnp.take(w_t, m_src, axis=0).T, 0.0)
    pg = jnp.stack([jnp.where(g_use, jnp.take(gdn_A_log, g_head), 0.0),
                    jnp.where(g_use, jnp.take(gdn_dt_bias, g_head), 0.0)])
    pm = jnp.where(m_use, jnp.where(slot_is_i, jnp.take(mlstm_b_i, m_head), jnp.take(mlstm_b_f, m_head)), 0.0)[None, :]

    x2 = x.reshape(t, d)
    rows = max(8, batch)
    c8 = jnp.pad(c, ((0, rows - batch), (0, 0)))
    mod = _ada(c8, w_ada, b_ada)[:batch].reshape(batch, 1, 3 * d)
    nw_pre = norm_pre_w.reshape(1, d)

    tm = min(1024, seq)
    p = _inproj(x2, mod, nw_pre, w_t, conv_big, seq, tm=tm, tn=tn, conv_tiles=conv_tiles, l2_tiles=l2_tiles,
                segments=segments)
    gcol, grow, mcol, mrow = _gates(x2, mod, nw_pre, jnp.concatenate([wg, wm], axis=1), pg, pm, batch, seq,
                                    tg=min(512, seq))
    grow = grow.reshape(batch, GDN_QK_HEADS, GDN_GATE_SLOTS, seq)
    mrow = mrow.reshape(batch, MLSTM_HEADS, MLSTM_GATE_SLOTS, seq)

    tt = 256
    ya = _gdn(p, gcol, grow, gdn_norm_w.reshape(1, GDN_HEAD_DIM), batch, seq, tt, 8,
              col_q, col_k, col_v, col_z)
    hb = _mlstm(p, mcol, mrow, mlstm_norm_w.reshape(1, mv_w), batch, seq, tt, 8,
                col_mq, col_mk, col_mv, col_mo, col_mz)
    mg = _merge(ya, hb, w_proj_gdn.astype(BF16), w_proj_mlstm.astype(BF16), p, col_ga, col_gb, tm=tm, tn=512)
    y = _out(mg, w_out.astype(BF16), x2, mod, norm_post_w.reshape(1, d), seq, tm=min(512, seq))
    return y.reshape(batch, seq, d)


def kernel(x, c, w_ada, b_ada, norm_pre_w, w_in, gdn_conv_w, gdn_A_log, gdn_dt_bias, gdn_norm_w, mlstm_conv_w, mlstm_b_i, mlstm_b_f, mlstm_norm_w, w_proj_gdn, w_proj_mlstm, w_out, norm_post_w):
    for l in range(w_ada.shape[0]):
        x = _layer(x, c, w_ada[l], b_ada[l], norm_pre_w[l], w_in[l], gdn_conv_w[l], gdn_A_log[l],
                   gdn_dt_bias[l], gdn_norm_w[l], mlstm_conv_w[l], mlstm_b_i[l], mlstm_b_f[l],
                   mlstm_norm_w[l], w_proj_gdn[l], w_proj_mlstm[l], w_out[l], norm_post_w[l])
    return x
```

```python
import functools

import jax
import jax.numpy as jnp
import numpy as np
from jax import lax
from jax.experimental import pallas as pl
from jax.experimental.pallas import tpu as pltpu

F32 = jnp.float32
BF16 = jnp.bfloat16

NORM_EPS = 1e-6
CHUNK = 64
CONV_WIDTH = 4
LANES = 128
F32_SUBLANES = 8
BF16_SUBLANES = 16
CONV_BLOCK = 64
INV_BASE = 8
GROUP = 128
GDN_WAVE = 8

GDN_QK_HEADS = 16
GDN_V_HEADS = 32
GDN_HEAD_DIM = 128
MLSTM_HEADS = 8
MLSTM_QK_DIM = 128
MLSTM_V_DIM = 256
GATE_LANES = 128
GDN_GATE_SLOTS = GATE_LANES // GDN_QK_HEADS
MLSTM_GATE_SLOTS = GATE_LANES // MLSTM_HEADS

VMEM_LIMIT_BYTES = 56 * 1024 * 1024


def _cparams(n_axes):
    return pltpu.CompilerParams(dimension_semantics=("arbitrary",) * n_axes,
                                vmem_limit_bytes=VMEM_LIMIT_BYTES)


def _mm(a, b):
    return jnp.dot(a.astype(BF16), b.astype(BF16), preferred_element_type=F32)


def _mm_nt(a, b):
    return lax.dot_general(a.astype(BF16), b.astype(BF16), (((1,), (1,)), ((), ())),
                           preferred_element_type=F32)


def _sigmoid(x):
    return 0.5 + 0.5 * jnp.tanh(0.5 * x)


def _silu(x):
    half = 0.5 * x
    return half + half * jnp.tanh(half)


def _softplus(x):
    return jnp.maximum(x, 0.0) + jnp.log(1.0 + jnp.exp(-jnp.abs(x)))


def _rms(x):
    return x * lax.rsqrt(jnp.mean(x * x, axis=-1, keepdims=True) + NORM_EPS)


def _prenorm(x, mod_ref, nw_ref, d):
    y = _rms(x) * nw_ref[...]
    return y * (1.0 + mod_ref[:, d:2 * d]) + mod_ref[:, 0:d]


def _ada_kernel(c_ref, w_ref, b_ref, o_ref):
    o_ref[...] = _mm(_silu(c_ref[...]), w_ref[...]) + b_ref[...]


def _ada(c8, w_ada, b_ada, tn=512):
    rows, d = c8.shape
    n = w_ada.shape[1]
    return pl.pallas_call(
        _ada_kernel,
        grid=(n // tn,),
        in_specs=[pl.BlockSpec((rows, d), lambda j: (0, 0)),
                  pl.BlockSpec((d, tn), lambda j: (0, j)),
                  pl.BlockSpec((1, tn), lambda j: (0, j))],
        out_specs=pl.BlockSpec((rows, tn), lambda j: (0, j)),
        out_shape=jax.ShapeDtypeStruct((rows, n), F32),
        compiler_params=_cparams(1),
    )(c8, w_ada, b_ada.reshape(1, n))


def _inproj_kernel(x_ref, mod_ref, nw_ref, w_ref, cw_ref, o_ref, h_ref, hperm_ref, tail_ref, unperm_ref, stage_ref,
                   *, d, tm, sub, per_seq, conv_tiles, l2_tiles):
    i, j = pl.program_id(0), pl.program_id(1)
    tn = o_ref.shape[1]
    slab = F32_SUBLANES
    n_slabs = CONV_BLOCK // slab

    @pl.when(j == 0)
    def _():
        def block(blk, carry):
            base = pl.multiple_of(blk * CONV_BLOCK, CONV_BLOCK)
            hb = _prenorm(x_ref[pl.ds(base, CONV_BLOCK), :], mod_ref, nw_ref, d)
            h_ref[pl.ds(base, CONV_BLOCK), :] = hb.astype(BF16)
            for lt in range(d // LANES):
                stage_ref[lt] = hb[:, lt * LANES:(lt + 1) * LANES]
            for s in range(0, n_slabs, 2):
                rows = jnp.concatenate(
                    [jnp.concatenate([stage_ref[lt, pl.ds(s + e, slab, stride=n_slabs), :]
                                      for lt in range(d // LANES)], axis=1) for e in range(2)], axis=0)
                hperm_ref[pl.ds(base + slab * s, 2 * slab), :] = rows.astype(BF16)
            return carry

        lax.fori_loop(0, tm // CONV_BLOCK, block, 0)

    is_conv = functools.reduce(jnp.logical_or, [(j >= lo) & (j < hi) for lo, hi in conv_tiles])

    def project(lhs_ref, r):
        return lax.dot_general(lhs_ref[r * sub:(r + 1) * sub, :], w_ref[...].astype(BF16),
                               (((1,), (1,)), ((), ())), preferred_element_type=F32)

    @pl.when(jnp.logical_not(is_conv))
    def _():
        for r in range(tm // sub):
            o_ref[r * sub:(r + 1) * sub, :] = project(h_ref, r).astype(BF16)

    is_l2 = j < l2_tiles
    l2_scale = jnp.where(j < l2_tiles // 2, LANES ** -0.5, 1.0)

    def conv_tile(l2):
        @pl.when((i % per_seq) == 0)
        def _():
            tail_ref[j] = jnp.zeros(tail_ref.shape[1:], F32)

        cw = cw_ref[...]
        top = lax.broadcasted_iota(jnp.int32, (slab, tn), 0) == 0
        keep = n_slabs - (CONV_WIDTH - 1)
        prev = [pltpu.roll(tail_ref[j, k * slab:(k + 1) * slab, :], 1, axis=0) for k in range(CONV_WIDTH - 1)]
        last = None
        for r in range(tm // sub):
            acc = project(hperm_ref, r)
            for blk in range(sub // CONV_BLOCK):
                row0 = blk * CONV_BLOCK
                slabs = [acc[row0 + s * slab:row0 + (s + 1) * slab, :] for s in range(n_slabs)]
                moved = [pltpu.roll(slabs[keep + k], 1, axis=0) for k in range(CONV_WIDTH - 1)]
                wrapped = [jnp.where(top, prev[k], moved[k]) for k in range(CONV_WIDTH - 1)]
                prev, last = moved, slabs
                for s in range(n_slabs):
                    half = cw[CONV_WIDTH - 1:CONV_WIDTH, :] * slabs[s]
                    for back in range(1, CONV_WIDTH):
                        src = slabs[s - back] if s >= back else wrapped[s - back + n_slabs - keep]
                        half = half + cw[CONV_WIDTH - 1 - back:CONV_WIDTH - back, :] * src
                    out = half + half * jnp.tanh(half)
                    for lt in range(tn // LANES):
                        head = out[:, lt * LANES:(lt + 1) * LANES]
                        if l2:
                            head = head * (lax.rsqrt(jnp.sum(head * head, axis=-1, keepdims=True) + NORM_EPS) * l2_scale)
                        unperm_ref[lt, pl.ds(row0 + s, slab, stride=n_slabs), :] = head
            for lt in range(tn // LANES):
                o_ref[r * sub:(r + 1) * sub, lt * LANES:(lt + 1) * LANES] = unperm_ref[lt].astype(BF16)
        tail_ref[j] = jnp.concatenate(last[keep:], axis=0)

    @pl.when(is_l2)
    def _():
        conv_tile(True)

    @pl.when(is_conv & jnp.logical_not(is_l2))
    def _():
        conv_tile(False)


def _inproj(x2, mod, nw, w_t, conv_big, seq, tm, tn, conv_tiles, l2_tiles, segments):
    t, d = x2.shape
    n = conv_big.shape[1]
    per_seq = seq // tm
    sub = min(256, tm)

    def w_row(j):
        units, prev = j * (tn // BF16_SUBLANES), 0
        for p_col, src_col in segments[1:]:
            delta = src_col - p_col
            assert p_col % tn == 0 and (delta - prev) % BF16_SUBLANES == 0
            units = units + (j >= p_col // tn).astype(jnp.int32) * ((delta - prev) // BF16_SUBLANES)
            prev = delta
        return units * BF16_SUBLANES

    return pl.pallas_call(
        functools.partial(_inproj_kernel, d=d, tm=tm, sub=sub, per_seq=per_seq, conv_tiles=conv_tiles,
                          l2_tiles=l2_tiles),
        grid=(t // tm, n // tn),
        in_specs=[pl.BlockSpec((tm, d), lambda i, j: (i, 0)),
                  pl.BlockSpec((None, 1, 3 * d), lambda i, j: (i // per_seq, 0, 0)),
                  pl.BlockSpec((1, d), lambda i, j: (0, 0)),
                  pl.BlockSpec((pl.Element(tn), pl.Element(d)), lambda i, j: (w_row(j), 0)),
                  pl.BlockSpec((8, tn), lambda i, j: (0, j))],
        out_specs=pl.BlockSpec((tm, tn), lambda i, j: (i, j)),
        out_shape=jax.ShapeDtypeStruct((t, n), BF16),
        scratch_shapes=[pltpu.VMEM((tm, d), BF16), pltpu.VMEM((tm, d), BF16),
                        pltpu.VMEM((n // tn, (CONV_WIDTH - 1) * F32_SUBLANES, tn), F32),
                        pltpu.VMEM((tn // LANES, sub, LANES), F32), pltpu.VMEM((d // LANES, CONV_BLOCK, LANES), F32)],
        compiler_params=_cparams(2),
    )(x2, mod, nw, w_t, conv_big)


def _same_block(ri, ci, size):
    shift = size.bit_length() - 1
    return (ri >> shift) == (ci >> shift)


def _gates_kernel(x_ref, mod_ref, nw_ref, whi_ref, wlo_ref, pg_ref, pm_ref,
                  gcol_ref, grow_ref, mcol_ref, mrow_ref, mstate_ref, *, d, tg, per_seq):
    h = _prenorm(x_ref[...], mod_ref, nw_ref, d)
    h_hi = h.astype(BF16)
    h_lo = (h - h_hi.astype(F32)).astype(BF16)
    w_hi, w_lo = whi_ref[...], wlo_ref[...]
    y = (jnp.dot(h_hi, w_hi, preferred_element_type=F32)
         + (jnp.dot(h_hi, w_lo, preferred_element_type=F32) + jnp.dot(h_lo, w_hi, preferred_element_type=F32)))
    yg, ym = y[:, :GATE_LANES], y[:, GATE_LANES:]
    lane = lax.broadcasted_iota(jnp.int32, (tg, GATE_LANES), 1)

    slot = lane & (GDN_GATE_SLOTS - 1)
    g = -jnp.exp(pg_ref[0:1, :]) * _softplus(yg + pg_ref[1:2, :])
    is_g = (slot == 0) | (slot == 1) | (slot == 4) | (slot == 5)
    is_beta = (slot == 2) | (slot == 3)
    y_g = jnp.where(is_g, g, jnp.where(is_beta, _sigmoid(yg), 0.0))

    mslot = lane & (MLSTM_GATE_SLOTS - 1)
    pre = ym + pm_ref[0:1, :]
    y_m = jnp.where(mslot == 0, pre, jnp.where((mslot == 1) | (mslot == 2), -_softplus(-pre), 0.0))

    ri = lax.broadcasted_iota(jnp.int32, (tg, tg), 0)
    ci = lax.broadcasted_iota(jnp.int32, (tg, tg), 1)
    same = _same_block(ri, ci, CHUNK)
    lower = jnp.where(same & (ri >= ci), 1.0, 0.0).astype(BF16)
    total = jnp.where(same, 1.0, 0.0).astype(BF16)
    y_all = jnp.concatenate([y_g, y_m], axis=1)
    parts = []
    rest = y_all
    for _ in range(3):
        piece = rest.astype(BF16)
        parts.append(piece)
        rest = rest - piece.astype(F32)
    cum = sum(jnp.dot(lower, piece, preferred_element_type=F32) for piece in reversed(parts))
    tot = sum(jnp.dot(total, piece, preferred_element_type=F32) for piece in reversed(parts))

    g_out = jnp.where(slot < 2, cum[:, :GATE_LANES],
                      jnp.where((slot == 4) | (slot == 5), tot[:, :GATE_LANES], y_g))
    m_out = jnp.where(mslot == 1, cum[:, GATE_LANES:],
                      jnp.where(mslot == 2, tot[:, GATE_LANES:], y_m))

    @pl.when(pl.program_id(0) % per_seq == 0)
    def _():
        mstate_ref[...] = jnp.zeros_like(mstate_ref)

    b_al = pltpu.roll(m_out, GATE_LANES - 1, axis=1)
    bl_al = pltpu.roll(m_out, GATE_LANES - 2, axis=1)
    log_end = bl_al - b_al + m_out
    m = mstate_ref[0:1, :]
    m_starts, m_nexts = [], []
    for c in range(tg // CHUNK):
        r0 = c * CHUNK
        m_starts.append(jnp.broadcast_to(m, (CHUNK, GATE_LANES)))
        m = jnp.maximum(bl_al[r0:r0 + 1, :] + m, jnp.max(log_end[r0:r0 + CHUNK, :], axis=0, keepdims=True))
        m_nexts.append(jnp.broadcast_to(m, (CHUNK, GATE_LANES)))
    mstate_ref[...] = jnp.broadcast_to(m, mstate_ref.shape)
    m_out = jnp.where(mslot == 3, pltpu.roll(jnp.concatenate(m_starts, axis=0), 3, axis=1),
                      jnp.where(mslot == 4, pltpu.roll(jnp.concatenate(m_nexts, axis=0), 4, axis=1), m_out))
    gcol_ref[...] = g_out
    grow_ref[...] = g_out.T
    mcol_ref[...] = m_out
    mrow_ref[...] = m_out.T


def _gates(x2, mod, nw, w_gates, pg, pm, batch, seq, tg):
    t, d = x2.shape
    per_seq = seq // tg
    w_hi = w_gates.astype(BF16)
    w_lo = (w_gates - w_hi.astype(F32)).astype(BF16)
    col = jax.ShapeDtypeStruct((t, GATE_LANES), F32)
    row = jax.ShapeDtypeStruct((batch, GATE_LANES, seq), F32)
    col_spec = pl.BlockSpec((tg, GATE_LANES), lambda i: (i, 0))
    row_spec = pl.BlockSpec((None, GATE_LANES, tg), lambda i: (i // per_seq, 0, i % per_seq))
    const = lambda shape: pl.BlockSpec(shape, lambda i: (0, 0))
    return pl.pallas_call(
        functools.partial(_gates_kernel, d=d, tg=tg, per_seq=per_seq),
        grid=(t // tg,),
        in_specs=[pl.BlockSpec((tg, d), lambda i: (i, 0)),
                  pl.BlockSpec((None, 1, 3 * d), lambda i: (i // per_seq, 0, 0)),
                  const((1, d)), const((d, 2 * GATE_LANES)), const((d, 2 * GATE_LANES)),
                  const((2, GATE_LANES)), const((1, GATE_LANES))],
        out_specs=[col_spec, row_spec, col_spec, row_spec],
        out_shape=[col, row, col, row],
        scratch_shapes=[pltpu.VMEM((8, GATE_LANES), F32)],
        compiler_params=_cparams(1),
    )(x2, mod, nw, w_hi, w_lo, pg, pm)


def _unit_lower_inverses(mats, ri, ci):
    diag = _same_block(ri, ci, INV_BASE)
    eye = jnp.where(ri == ci, 1.0, 0.0)
    pows = [jnp.where(diag, a, 0.0) for a in mats]
    invs = [eye - ab for ab in pows]
    order = 2
    while order < INV_BASE:
        pows = [_mm(ab, ab) for ab in pows]
        yield
        invs = [inv + _mm(inv, ab) for inv, ab in zip(invs, pows)]
        yield
        order *= 2
    size = INV_BASE
    while size < CHUNK:
        merged = _same_block(ri, ci, 2 * size)
        keep = merged & jnp.logical_not(diag)
        corr = [_mm(_odd_blocks(jnp.where(keep, a, 0.0), size), inv) for a, inv in zip(mats, invs)]
        yield
        zeros = jnp.zeros_like(invs[0])
        invs = [_with_odd_blocks(inv, _odd_blocks(inv, size) - _mm(_odd_blocks(inv, size),
                                                                     _with_odd_blocks(zeros, cr, size)), size)
                for inv, cr in zip(invs, corr)]
        yield
        diag, size = merged, 2 * size
    return invs


def _odd_blocks(x, size):
    return jnp.concatenate([x[b * size:(b + 1) * size] for b in range(1, x.shape[0] // size, 2)], axis=0)


def _with_odd_blocks(x, odd, size):
    blocks = [odd[(b // 2) * size:(b // 2 + 1) * size] if b % 2 else x[b * size:(b + 1) * size]
              for b in range(x.shape[0] // size)]
    return jnp.concatenate(blocks, axis=0)


def _interleave(*gens):
    live = list(gens)
    while live:
        for gen in list(live):
            try:
                next(gen)
            except StopIteration:
                live.remove(gen)


def _lane_slot_shift(group, slots):
    return lax.rem((GATE_LANES // slots - group) * slots, GATE_LANES)


def _group_masks():
    ri = lax.broadcasted_iota(jnp.int32, (GROUP, GROUP), 0)
    ci = lax.broadcasted_iota(jnp.int32, (GROUP, GROUP), 1)
    same = _same_block(ri, ci, CHUNK)
    return ri, ci, same & (ri >= ci), same & (ri > ci)


def _rows(x, g):
    return x[g * GROUP:(g + 1) * GROUP]


def _lanes(x, g):
    return x[:, g * GROUP:(g + 1) * GROUP]


def _gdn_kernel(q_ref, k_ref, v_ref, z_ref, gcol_ref, grow_ref, nw_ref, o_ref, s_ref, *, tt, hps):
    hd = GDN_HEAD_DIM
    n_groups, n_chunks, per_group = tt // GROUP, tt // CHUNK, GROUP // CHUNK

    @pl.when(pl.program_id(2) == 0)
    def _():
        s_ref[...] = jnp.zeros_like(s_ref)

    gcol = pltpu.roll(gcol_ref[...], _lane_slot_shift(pl.program_id(1) * hps, GDN_GATE_SLOTS), axis=1)
    ri, ci, causal, strict = _group_masks()
    n_waves = hps // GDN_WAVE
    waves = [dict(pairs=range(w * GDN_WAVE, (w + 1) * GDN_WAVE), heads=[], a_mats=[], done=[]) for w in range(n_waves)]

    def prep(w):
        return _gdn_prep(waves[w], q_ref, k_ref, v_ref, gcol, grow_ref, causal, strict, n_groups)

    def solve(w):
        return _gdn_solve(waves[w], s_ref, ri, ci, n_groups, n_chunks, per_group)

    def finish(w):
        for vh, state, out in waves[w]["done"]:
            s_ref[vh] = state
            o = _rms(out) * nw_ref[...]
            z = z_ref[:, vh * hd:(vh + 1) * hd].astype(F32)
            o_ref[:, vh * hd:(vh + 1) * hd] = (o * _silu(z)).astype(BF16)
            yield

    def wy(w):
        return _gdn_wy(waves[w], ri, ci, n_groups)

    def scan(w):
        return _gdn_scan(waves[w], s_ref, n_chunks, per_group)

    _interleave(prep(0))
    for w in range(n_waves + 1):
        gens = ([wy(w)] if w < n_waves else []) + ([prep(w + 1)] if w + 1 < n_waves else [])
        gens += ([scan(w - 1)] if w >= 1 else []) + ([finish(w - 2)] if w >= 2 else [])
        _interleave(*gens)
    _interleave(finish(n_waves - 1))


def _gdn_prep(wave, q_ref, k_ref, v_ref, gcol, grow_ref, causal, strict, n_groups):
    hd = GDN_HEAD_DIM
    heads, a_mats = wave["heads"], wave["a_mats"]
    for pp in wave["pairs"]:
        q16, k16 = q_ref[:, pp * hd:(pp + 1) * hd], k_ref[:, pp * hd:(pp + 1) * hd]
        q, k_t = q16.astype(F32), k16.astype(F32).T
        kk = [_mm_nt(_rows(k16, g), _rows(k16, g)) for g in range(n_groups)]
        qk = [_mm_nt(_rows(q16, g), _rows(k16, g)) for g in range(n_groups)]
        yield
        for hh in range(2):
            lane = pp * GDN_GATE_SLOTS + hh
            gc_c, be_c = gcol[:, lane:lane + 1], gcol[:, lane + 2:lane + 3]
            gc_r, be_r, gl_r = grow_ref[pp, hh:hh + 1, :], grow_ref[pp, 2 + hh:3 + hh, :], grow_ref[pp, 4 + hh:5 + hh, :]
            decay = [jnp.exp(jnp.where(causal, _rows(gc_c, g) - _lanes(gc_r, g), -jnp.inf)) for g in range(n_groups)]
            a_mats += [jnp.where(strict, _rows(be_c, g) * kk[g] * decay[g], 0.0) for g in range(n_groups)]
            vh = 2 * pp + hh
            heads.append(dict(vh=vh, k16=k16, qk=qk, decay=decay, t_scale=be_r, w_scale=be_r * jnp.exp(gc_r),
                              qd=q * jnp.exp(gc_c), kt_t=(k_t * jnp.exp(gl_r - gc_r)).astype(BF16),
                              g_tot=jnp.exp(gl_r), v=v_ref[:, vh * hd:(vh + 1) * hd]))
            yield


def _gdn_solve(wave, s_ref, ri, ci, n_groups, n_chunks, per_group):
    hd = GDN_HEAD_DIM
    heads = wave["heads"]
    invs = yield from _unit_lower_inverses(wave["a_mats"], ri, ci)

    for hi, h in enumerate(heads):
        h["wu"] = []
        for g in range(n_groups):
            inv = invs[hi * n_groups + g]
            w = _mm(inv * _lanes(h["w_scale"], g), _rows(h["k16"], g))
            u = _mm(inv * _lanes(h["t_scale"], g), _rows(h["v"], g))
            h["wu"].append(jnp.concatenate([w, u], axis=1).astype(BF16))
        yield
    for h in heads:
        h["awu"] = [_mm(h["qk"][g] * h["decay"][g], h["wu"][g]) for g in range(n_groups)]
    yield
    for h in heads:
        h["ktwu"] = []
        for c in range(n_chunks):
            g, r0 = c // per_group, (c % per_group) * CHUNK
            h["ktwu"].append(_mm(h["kt_t"][:, c * CHUNK:(c + 1) * CHUNK], h["wu"][g][r0:r0 + CHUNK]))
        yield

    states = [s_ref[h["vh"]] for h in heads]
    outs = [[] for _ in heads]
    for c in range(n_chunks):
        g, r0 = c // per_group, (c % per_group) * CHUNK
        for hi, h in enumerate(heads):
            aw, au = h["awu"][g][r0:r0 + CHUNK, 0:hd], h["awu"][g][r0:r0 + CHUNK, hd:2 * hd]
            q_eff = h["qd"][c * CHUNK:(c + 1) * CHUNK] - aw
            prod = _mm(jnp.concatenate([h["ktwu"][c][:, 0:hd], q_eff], axis=0), states[hi])
            outs[hi].append(prod[hd:hd + CHUNK] + au)
            states[hi] = (states[hi] * h["g_tot"][:, c * CHUNK:c * CHUNK + 1] - prod[0:hd]
                          + h["ktwu"][c][:, hd:2 * hd])
        yield
    wave["done"].extend((h["vh"], states[hi], jnp.concatenate(outs[hi], axis=0)) for hi, h in enumerate(heads))


def _gdn_wy(wave, ri, ci, n_groups):
    heads = wave["heads"]
    invs = yield from _unit_lower_inverses(wave["a_mats"], ri, ci)
    for hi, h in enumerate(heads):
        h["w"], h["u"], h["attn"] = [], [], []
        for g in range(n_groups):
            inv = invs[hi * n_groups + g]
            h["w"].append(_mm(inv * _lanes(h["w_scale"], g), _rows(h["k16"], g)))
            h["u"].append(_mm(inv * _lanes(h["t_scale"], g), _rows(h["v"], g)))
            h["attn"].append((h["qk"][g] * h["decay"][g]).astype(BF16))
        yield


def _gdn_scan(wave, s_ref, n_chunks, per_group):
    hd = GDN_HEAD_DIM
    heads = wave["heads"]
    states = [s_ref[h["vh"]] for h in heads]
    outs = [[] for _ in heads]
    zeros = jnp.zeros((CHUNK, hd), F32)
    for c in range(n_chunks):
        g, pos = c // per_group, c % per_group
        r0, rows = pos * CHUNK, slice(c * CHUNK, (c + 1) * CHUNK)
        prods = [_mm(jnp.concatenate([h["w"][g][r0:r0 + CHUNK], h["qd"][rows]], axis=0), states[hi])
                 for hi, h in enumerate(heads)]
        yield
        for hi, h in enumerate(heads):
            v_new = h["u"][g][r0:r0 + CHUNK] - prods[hi][0:CHUNK]
            v_group = jnp.concatenate([v_new if b == pos else zeros for b in range(per_group)], axis=0)
            outs[hi].append(prods[hi][CHUNK:2 * CHUNK] + _mm(h["attn"][g][r0:r0 + CHUNK], v_group))
            states[hi] = states[hi] * h["g_tot"][:, c * CHUNK:c * CHUNK + 1] + _mm(h["kt_t"][:, rows], v_new)
        yield
    wave["done"].extend((h["vh"], states[hi], jnp.concatenate(outs[hi], axis=0)) for hi, h in enumerate(heads))


def _gdn(p,gcol, grow, norm_w, batch, seq, tt, hps, col_q, col_k, col_v, col_z):
    t = batch * seq
    hd = GDN_HEAD_DIM
    qw, vw = hps * hd, 2 * hps * hd
    per_seq = seq // tt
    rowi = lambda b, h, i: b * per_seq + i
    return pl.pallas_call(
        functools.partial(_gdn_kernel, tt=tt, hps=hps),
        grid=(batch, GDN_QK_HEADS // hps, per_seq),
        in_specs=[
            pl.BlockSpec((tt, qw), lambda b, h, i: (rowi(b, h, i), col_q // qw + h)),
            pl.BlockSpec((tt, qw), lambda b, h, i: (rowi(b, h, i), col_k // qw + h)),
            pl.BlockSpec((tt, vw), lambda b, h, i: (rowi(b, h, i), col_v // vw + h)),
            pl.BlockSpec((tt, vw), lambda b, h, i: (rowi(b, h, i), col_z // vw + h)),
            pl.BlockSpec((tt, GATE_LANES), lambda b, h, i: (rowi(b, h, i), 0)),
            pl.BlockSpec((None, hps, GDN_GATE_SLOTS, tt), lambda b, h, i: (b, h, 0, i)),
            pl.BlockSpec((1, hd), lambda b, h, i: (0, 0)),
        ],
        out_specs=pl.BlockSpec((tt, vw), lambda b, h, i: (rowi(b, h, i), h)),
        out_shape=jax.ShapeDtypeStruct((t, GDN_V_HEADS * hd), BF16),
        scratch_shapes=[pltpu.VMEM((2 * hps, hd, hd), F32)],
        compiler_params=_cparams(3),
    )(p, p, p, p, gcol, grow, norm_w)


def _mlstm_kernel(q_ref, k_ref, v_ref, og_ref, z_ref, mcol_ref, mrow_ref, nw_ref, o_ref, c_ref, *, tt, hps):
    dk, dv = MLSTM_QK_DIM, MLSTM_V_DIM
    q_scale = dk ** -0.5
    n_groups, n_chunks = tt // GROUP, tt // CHUNK

    @pl.when(pl.program_id(2) == 0)
    def _():
        c_ref[...] = jnp.zeros_like(c_ref)

    mcol = pltpu.roll(mcol_ref[...], _lane_slot_shift(pl.program_id(1) * hps, MLSTM_GATE_SLOTS), axis=1)
    _, _, causal, _ = _group_masks()
    lane = lax.broadcasted_iota(jnp.int32, (tt, 128), 1)
    ones_col = jnp.where(lane == 0, 1.0, 0.0).astype(BF16)

    heads = []
    for hh in range(hps):
        q16, k16 = q_ref[:, hh * dk:(hh + 1) * dk], k_ref[:, hh * dk:(hh + 1) * dk]
        v_aug = jnp.concatenate([v_ref[:, hh * dv:(hh + 1) * dv], ones_col], axis=1)
        lane0 = hh * MLSTM_GATE_SLOTS
        b_c, m_start_c = mcol[:, lane0 + 1:lane0 + 2], mcol[:, lane0 + 3:lane0 + 4]
        i_r, b_r, bl_r = mrow_ref[hh, 0:1, :], mrow_ref[hh, 1:2, :], mrow_ref[hh, 2:3, :]
        m_start_r, m_next_r = mrow_ref[hh, 3:4, :], mrow_ref[hh, 4:5, :]
        log_end = bl_r - b_r + i_r
        k_ts =(k16.astype(F32).T * jnp.exp(log_end - m_next_r)).astype(BF16)
        heads.append(dict(q16=q16, k16=k16, v_aug=v_aug, b_c=b_c, b_r=b_r, i_r=i_r, m_inter=b_c + m_start_c,
                          k_ts=k_ts, carry=jnp.exp(bl_r + m_start_r - m_next_r)))

    probs = [(h, g) for h in heads for g in range(n_groups)]
    qks = [_mm_nt(_rows(h["q16"], g), _rows(h["k16"], g)) for h, g in probs]
    log_ds = [jnp.where(causal, _rows(h["b_c"], g) - _lanes(h["b_r"], g) + _lanes(h["i_r"], g), -jnp.inf)
              for h, g in probs]
    row_max = [jnp.max(ld, axis=1, keepdims=True) for ld in log_ds]
    m_ts = [jnp.maximum(_rows(h["m_inter"], g), rm) for (h, g), rm in zip(probs, row_max)]
    weights = [qk * (q_scale * jnp.exp(ld - m_t)) for qk, ld, m_t in zip(qks, log_ds, m_ts)]
    intras = [_mm(s, _rows(h["v_aug"], g)) for (h, g), s in zip(probs, weights)]
    for hi, h in enumerate(heads):
        h["m_t"] = jnp.concatenate(m_ts[hi * n_groups:(hi + 1) * n_groups], axis=0)
        h["intra"] = jnp.concatenate(intras[hi * n_groups:(hi + 1) * n_groups], axis=0)
    for h in heads:
        h["kv"] = [_mm(h["k_ts"][:, c * CHUNK:(c + 1) * CHUNK], h["v_aug"][c * CHUNK:(c + 1) * CHUNK])
                   for c in range(n_chunks)]

    for hh, h in enumerate(heads):
        state = c_ref[hh]
        inters = []
        for c in range(n_chunks):
            inters.append(_mm(h["q16"][c * CHUNK:(c + 1) * CHUNK], state))
            state = state * h["carry"][:, c * CHUNK:c * CHUNK + 1] + h["kv"][c]
        c_ref[hh] = state
        m_t = h["m_t"]
        num = (q_scale * jnp.exp(h["m_inter"] - m_t)) * jnp.concatenate(inters, axis=0) + h["intra"]
        den = jnp.maximum(jnp.abs(num[:, dv:dv + 1]), jnp.exp(-m_t))
        hn = _rms(num[:, 0:dv] / den) * nw_ref[:, hh * dv:(hh + 1) * dv]
        cols = slice(hh * dv, (hh + 1) * dv)
        o_ref[:, cols] = (_sigmoid(og_ref[:, cols].astype(F32)) * hn * _silu(z_ref[:, cols].astype(F32))).astype(BF16)


def _mlstm(p, mcol, mrow, norm_w, batch, seq, tt, hps, col_q, col_k, col_v, col_o, col_z):
    t = batch * seq
    dk, dv = MLSTM_QK_DIM, MLSTM_V_DIM
    qw, vw = hps * dk, hps * dv
    per_seq = seq // tt
    rowi = lambda b, h, i: b * per_seq + i
    return pl.pallas_call(
        functools.partial(_mlstm_kernel, tt=tt, hps=hps),
        grid=(batch, MLSTM_HEADS // hps, per_seq),
        in_specs=[
            pl.BlockSpec((tt, qw), lambda b, h, i: (rowi(b, h, i), col_q // qw + h)),
            pl.BlockSpec((tt, qw), lambda b, h, i: (rowi(b, h, i), col_k // qw + h)),
            pl.BlockSpec((tt, vw), lambda b, h, i: (rowi(b, h, i), col_v // vw + h)),
            pl.BlockSpec((tt, vw), lambda b, h, i: (rowi(b, h, i), col_o // vw + h)),
            pl.BlockSpec((tt, vw), lambda b, h, i: (rowi(b, h, i), col_z // vw + h)),
            pl.BlockSpec((tt, GATE_LANES), lambda b, h, i: (rowi(b, h, i), 0)),
            pl.BlockSpec((None, hps, MLSTM_GATE_SLOTS, tt), lambda b, h, i: (b, h, 0, i)),
            pl.BlockSpec((1, vw), lambda b, h, i: (0, h)),
        ],
        out_specs=pl.BlockSpec((tt, vw), lambda b, h, i: (rowi(b, h, i), h)),
        out_shape=jax.ShapeDtypeStruct((t, MLSTM_HEADS * dv), BF16),
        scratch_shapes=[pltpu.VMEM((hps, dk, dv + 128), F32)],
        compiler_params=_cparams(3),
    )(p, p, p, p, p, mcol, mrow, norm_w)


def _merge_kernel(ya_ref, hb_ref, wa_ref, wb_ref, ga_ref, gb_ref, o_ref):
    ya = jnp.dot(ya_ref[...], wa_ref[...], preferred_element_type=F32)
    yb = jnp.dot(hb_ref[...], wb_ref[...], preferred_element_type=F32)
    merged = _sigmoid(ga_ref[...].astype(F32)) * ya + _sigmoid(gb_ref[...].astype(F32)) * yb
    o_ref[...] = merged.astype(BF16)


def _merge(ya, hb, wa, wb, p, col_ga, col_gb, tm, tn):
    t = ya.shape[0]
    d = wa.shape[1]
    return pl.pallas_call(
        _merge_kernel,
        grid=(t // tm, d // tn),
        in_specs=[pl.BlockSpec((tm, ya.shape[1]), lambda i, j: (i, 0)),
                  pl.BlockSpec((tm, hb.shape[1]), lambda i, j: (i, 0)),
                  pl.BlockSpec((wa.shape[0], tn), lambda i, j: (0, j)),
                  pl.BlockSpec((wb.shape[0], tn), lambda i, j: (0, j)),
                  pl.BlockSpec((tm, tn), lambda i, j: (i, col_ga // tn + j)),
                  pl.BlockSpec((tm, tn), lambda i, j: (i, col_gb // tn + j))],
        out_specs=pl.BlockSpec((tm, tn), lambda i, j: (i, j)),
        out_shape=jax.ShapeDtypeStruct((t, d), BF16),
        compiler_params=_cparams(2),
    )(ya, hb, wa, wb, p, p)


def _out_kernel(mg_ref, w_ref, x_ref, mod_ref, nw_ref, o_ref, *, d):
    out = jnp.dot(mg_ref[...], w_ref[...], preferred_element_type=F32)
    o_ref[...] = x_ref[...] + mod_ref[:, 2 * d:3 * d] * (_rms(out) * nw_ref[...])


def _out(mg, w_out, x2, mod, nw, seq, tm):
    t, d = x2.shape
    per_seq = seq // tm
    return pl.pallas_call(
        functools.partial(_out_kernel, d=d),
        grid=(t // tm,),
        in_specs=[pl.BlockSpec((tm, d), lambda i: (i, 0)),
                  pl.BlockSpec((d, d), lambda i: (0, 0)),
                  pl.BlockSpec((tm, d), lambda i: (i, 0)),
                  pl.BlockSpec((None, 1, 3 * d), lambda i: (i // per_seq, 0, 0)),
                  pl.BlockSpec((1, d), lambda i: (0, 0))],
        out_specs=pl.BlockSpec((tm, d), lambda i: (i, 0)),
        out_shape=jax.ShapeDtypeStruct((t, d), F32),
        compiler_params=_cparams(1),
    )(mg, w_out, x2, mod, nw)


def _gate_columns(d):
    qkv = 2 * GDN_QK_HEADS * GDN_HEAD_DIM + GDN_V_HEADS * GDN_HEAD_DIM
    col_a, col_b = qkv, qkv + GDN_V_HEADS
    col_i = col_b + GDN_V_HEADS + GDN_V_HEADS * GDN_HEAD_DIM + 2 * MLSTM_HEADS * MLSTM_QK_DIM + MLSTM_HEADS * MLSTM_V_DIM
    col_f = col_i + MLSTM_HEADS
    g_src, g_head, g_use = np.zeros(GATE_LANES, np.int32), np.zeros(GATE_LANES, np.int32), np.zeros(GATE_LANES, bool)
    m_src, m_head, m_use = np.zeros(GATE_LANES, np.int32), np.zeros(GATE_LANES, np.int32), np.zeros(GATE_LANES, bool)
    for lane in range(GATE_LANES):
        pair, slot = divmod(lane, GDN_GATE_SLOTS)
        if slot < 6:
            head = 2 * pair + slot % 2
            g_src[lane] = (col_b if slot in (2, 3) else col_a) + head
            g_head[lane], g_use[lane] = head, True
        head, slot = divmod(lane, MLSTM_GATE_SLOTS)
        if slot < 3:
            m_src[lane] = (col_i if slot == 0 else col_f) + head
            m_head[lane], m_use[lane] = head, True
    slot_is_i = (np.arange(GATE_LANES) % MLSTM_GATE_SLOTS) == 0
    return (g_src, g_head, g_use), (m_src, m_head, m_use), slot_is_i


def _layer(x, c, w_ada, b_ada, norm_pre_w, w_in, gdn_conv_w, gdn_A_log, gdn_dt_bias, gdn_norm_w,
           mlstm_conv_w, mlstm_b_i, mlstm_b_f, mlstm_norm_w, w_proj_gdn, w_proj_mlstm, w_out, norm_post_w):
    batch, seq, d = x.shape
    t = batch * seq
    qk_w = GDN_QK_HEADS * GDN_HEAD_DIM
    v_w = GDN_V_HEADS * GDN_HEAD_DIM
    mqk_w = MLSTM_HEADS * MLSTM_QK_DIM
    mv_w = MLSTM_HEADS * MLSTM_V_DIM
    assert d == qk_w == mv_w and seq % 256 == 0

    src_a = 2 * qk_w + v_w
    src_z = src_a + 2 * GDN_V_HEADS
    src_mqk = src_z + v_w
    src_i = src_mqk + 2 * mqk_w + mv_w
    src_o = src_i + 2 * MLSTM_HEADS
    col_q, col_k, col_v, col_z = 0, qk_w, 2 * qk_w, 2 * qk_w + v_w
    col_mq = col_z + v_w
    col_mk, col_mv = col_mq + mqk_w, col_mq + 2 * mqk_w
    col_mo = col_mv + mv_w
    col_mz, col_ga, col_gb = col_mo + mv_w, col_mo + 2 * mv_w, col_mo + 2 * mv_w + d
    n_p = col_gb + d
    tn = 1024
    segments = ((0, 0), (col_z, src_z), (col_mo, src_o))
    assert all(p_col % tn == 0 for p_col, _ in segments) and n_p - col_mo == w_in.shape[1] - src_o
    w_t = jnp.swapaxes(w_in, 0, 1)

    conv_big = jnp.zeros((8, n_p), F32).at[0:CONV_WIDTH, col_q:col_z].set(0.5 * gdn_conv_w)
    conv_big = conv_big.at[0:CONV_WIDTH, col_mq:col_mv].set(0.5 * mlstm_conv_w)
    assert col_z % tn == 0 and col_mq % tn == 0 and col_mv % tn == 0
    conv_tiles = ((col_q // tn, col_z // tn), (col_mq // tn, col_mv // tn))
    assert col_q == 0 and col_k % tn == 0 and col_v == 2 * col_k and GDN_HEAD_DIM == LANES
    l2_tiles = col_v // tn

    (g_src, g_head, g_use), (m_src, m_head, m_use), slot_is_i = _gate_columns(d)
    wg = jnp.where(g_use[None, :], jnp.take(w_t, g_src, axis=0).T, 0.0)
    wm = jnp.where(m_use[None, :], jnp.take(w_t, m_src, axis=0).T, 0.0)
    pg = jnp.stack([jnp.where(g_use, jnp.take(gdn_A_log, g_head), 0.0),
                    jnp.where(g_use, jnp.take(gdn_dt_bias, g_head), 0.0)])
    pm = jnp.where(m_use, jnp.where(slot_is_i, jnp.take(mlstm_b_i, m_head), jnp.take(mlstm_b_f, m_head)), 0.0)[None, :]

    x2 = x.reshape(t, d)
    rows = max(8, batch)
    c8 = jnp.pad(c, ((0, rows - batch), (0, 0)))
    mod = _ada(c8, w_ada, b_ada)[:batch].reshape(batch, 1, 3 * d)
    nw_pre = norm_pre_w.reshape(1, d)

    tm = min(1024, seq)
    p = _inproj(x2, mod, nw_pre, w_t, conv_big, seq, tm=tm, tn=tn, conv_tiles=conv_tiles, l2_tiles=l2_tiles,
                segments=segments)
    gcol, grow, mcol, mrow = _gates(x2, mod, nw_pre, jnp.concatenate([wg, wm], axis=1), pg, pm, batch, seq,
                                    tg=min(512, seq))
    grow = grow.reshape(batch, GDN_QK_HEADS, GDN_GATE_SLOTS, seq)
    mrow = mrow.reshape(batch, MLSTM_HEADS, MLSTM_GATE_SLOTS, seq)

    tt = 256
    ya = _gdn(p, gcol, grow, gdn_norm_w.reshape(1, GDN_HEAD_DIM), batch, seq, tt, 16,
              col_q, col_k, col_v, col_z)
    hb = _mlstm(p, mcol, mrow, mlstm_norm_w.reshape(1, mv_w), batch, seq, tt, 8,
                col_mq, col_mk, col_mv, col_mo, col_mz)
    mg = _merge(ya, hb, w_proj_gdn.astype(BF16), w_proj_mlstm.astype(BF16), p, col_ga, col_gb, tm=tm, tn=512)
    y = _out(mg, w_out.astype(BF16), x2, mod, norm_post_w.reshape(1, d), seq, tm=min(512, seq))
    return y.reshape(batch, seq, d)


def kernel(x, c, w_ada, b_ada, norm_pre_w, w_in, gdn_conv_w, gdn_A_log, gdn_dt_bias, gdn_norm_w, mlstm_conv_w, mlstm_b_i, mlstm_b_f, mlstm_norm_w, w_proj_gdn, w_proj_mlstm, w_out, norm_post_w):
    for l in range(w_ada.shape[0]):
        x = _layer(x, c, w_ada[l], b_ada[l], norm_pre_w[l], w_in[l], gdn_conv_w[l], gdn_A_log[l],
                   gdn_dt_bias[l], gdn_norm_w[l], mlstm_conv_w[l], mlstm_b_i[l], mlstm_b_f[l],
                   mlstm_norm_w[l], w_proj_gdn[l], w_proj_mlstm[l], w_out[l], norm_post_w[l])
    return x
```

```python
import functools

import jax
import jax.numpy as jnp
import numpy as np
from jax import lax
from jax.experimental import pallas as pl
from jax.experimental.pallas import tpu as pltpu

F32 = jnp.float32
BF16 = jnp.bfloat16

NORM_EPS = 1e-6
CHUNK = 64
CONV_WIDTH = 4
LANES = 128
F32_SUBLANES = 8
BF16_SUBLANES = 16
CONV_BLOCK = 64
INV_BASE = 8
GROUP = 128
GDN_WAVE = 8
MLSTM_WAVE = 8

GDN_QK_HEADS = 16
GDN_V_HEADS = 32
GDN_HEAD_DIM = 128
MLSTM_HEADS = 8
MLSTM_QK_DIM = 128
MLSTM_V_DIM = 256
GATE_LANES = 128
GDN_GATE_SLOTS = GATE_LANES // GDN_QK_HEADS
MLSTM_GATE_SLOTS = GATE_LANES // MLSTM_HEADS

VMEM_LIMIT_BYTES = 56 * 1024 * 1024


def _cparams(n_axes):
    return pltpu.CompilerParams(dimension_semantics=("arbitrary",) * n_axes,
                                vmem_limit_bytes=VMEM_LIMIT_BYTES)


def _mm(a, b):
    return jnp.dot(a.astype(BF16), b.astype(BF16), preferred_element_type=F32)


def _mm_nt(a, b):
    return lax.dot_general(a.astype(BF16), b.astype(BF16), (((1,), (1,)), ((), ())),
                           preferred_element_type=F32)


def _sigmoid(x):
    return 0.5 + 0.5 * jnp.tanh(0.5 * x)


def _silu(x):
    half = 0.5 * x
    return half + half * jnp.tanh(half)


def _softplus(x):
    return jnp.maximum(x, 0.0) + jnp.log(1.0 + jnp.exp(-jnp.abs(x)))


def _rms(x):
    return x * lax.rsqrt(jnp.mean(x * x, axis=-1, keepdims=True) + NORM_EPS)


def _prenorm(x, mod_ref, nw_ref, d):
    y = _rms(x) * nw_ref[...]
    return y * (1.0 + mod_ref[:, d:2 * d]) + mod_ref[:, 0:d]


def _ada_kernel(c_ref, w_ref, b_ref, o_ref):
    o_ref[...] = _mm(_silu(c_ref[...]), w_ref[...]) + b_ref[...]


def _ada(c8, w_ada, b_ada, tn=512):
    rows, d = c8.shape
    n = w_ada.shape[1]
    return pl.pallas_call(
        _ada_kernel,
        grid=(n // tn,),
        in_specs=[pl.BlockSpec((rows, d), lambda j: (0, 0)),
                  pl.BlockSpec((d, tn), lambda j: (0, j)),
                  pl.BlockSpec((1, tn), lambda j: (0, j))],
        out_specs=pl.BlockSpec((rows, tn), lambda j: (0, j)),
        out_shape=jax.ShapeDtypeStruct((rows, n), F32),
        compiler_params=_cparams(1),
    )(c8, w_ada, b_ada.reshape(1, n))


def _inproj_kernel(x_ref, mod_ref, nw_ref, w_ref, cw_ref, o_ref, h_ref, hperm_ref, tail_ref, unperm_ref, stage_ref,
                   *, d, tm, sub, per_seq, conv_tiles, l2_tiles):
    i, j = pl.program_id(0), pl.program_id(1)
    tn = o_ref.shape[1]
    slab = F32_SUBLANES
    n_slabs = CONV_BLOCK // slab

    @pl.when(j == 0)
    def _():
        def block(blk, carry):
            base = pl.multiple_of(blk * CONV_BLOCK, CONV_BLOCK)
            hb = _prenorm(x_ref[pl.ds(base, CONV_BLOCK), :], mod_ref, nw_ref, d)
            h_ref[pl.ds(base, CONV_BLOCK), :] = hb.astype(BF16)
            for lt in range(d // LANES):
                stage_ref[lt] = hb[:, lt * LANES:(lt + 1) * LANES]
            for s in range(0, n_slabs, 2):
                rows = jnp.concatenate(
                    [jnp.concatenate([stage_ref[lt, pl.ds(s + e, slab, stride=n_slabs), :]
                                      for lt in range(d // LANES)], axis=1) for e in range(2)], axis=0)
                hperm_ref[pl.ds(base + slab * s, 2 * slab), :] = rows.astype(BF16)
            return carry

        lax.fori_loop(0, tm // CONV_BLOCK, block, 0)

    is_conv = functools.reduce(jnp.logical_or, [(j >= lo) & (j < hi) for lo, hi in conv_tiles])

    def project(lhs_ref, r):
        return lax.dot_general(lhs_ref[r * sub:(r + 1) * sub, :], w_ref[...].astype(BF16),
                               (((1,), (1,)), ((), ())), preferred_element_type=F32)

    @pl.when(jnp.logical_not(is_conv))
    def _():
        for r in range(tm // sub):
            o_ref[r * sub:(r + 1) * sub, :] = project(h_ref, r).astype(BF16)

    is_l2 = j < l2_tiles
    l2_scale = jnp.where(j < l2_tiles // 2, LANES ** -0.5, 1.0)

    def conv_tile(l2):
        @pl.when((i % per_seq) == 0)
        def _():
            tail_ref[j] = jnp.zeros(tail_ref.shape[1:], F32)

        cw = cw_ref[...]
        top = lax.broadcasted_iota(jnp.int32, (slab, tn), 0) == 0
        keep = n_slabs - (CONV_WIDTH - 1)
        prev = [pltpu.roll(tail_ref[j, k * slab:(k + 1) * slab, :], 1, axis=0) for k in range(CONV_WIDTH - 1)]
        last = None
        for r in range(tm // sub):
            acc = project(hperm_ref, r)
            for blk in range(sub // CONV_BLOCK):
                row0 = blk * CONV_BLOCK
                slabs = [acc[row0 + s * slab:row0 + (s + 1) * slab, :] for s in range(n_slabs)]
                moved = [pltpu.roll(slabs[keep + k], 1, axis=0) for k in range(CONV_WIDTH - 1)]
                wrapped = [jnp.where(top, prev[k], moved[k]) for k in range(CONV_WIDTH - 1)]
                prev, last = moved, slabs
                for s in range(n_slabs):
                    half = cw[CONV_WIDTH - 1:CONV_WIDTH, :] * slabs[s]
                    for back in range(1, CONV_WIDTH):
                        src = slabs[s - back] if s >= back else wrapped[s - back + n_slabs - keep]
                        half = half + cw[CONV_WIDTH - 1 - back:CONV_WIDTH - back, :] * src
                    out = half + half * jnp.tanh(half)
                    for lt in range(tn // LANES):
                        head = out[:, lt * LANES:(lt + 1) * LANES]
                        if l2:
                            head = head * (lax.rsqrt(jnp.sum(head * head, axis=-1, keepdims=True) + NORM_EPS) * l2_scale)
                        unperm_ref[lt, pl.ds(row0 + s, slab, stride=n_slabs), :] = head
            for lt in range(tn // LANES):
                o_ref[r * sub:(r + 1) * sub, lt * LANES:(lt + 1) * LANES] = unperm_ref[lt].astype(BF16)
        tail_ref[j] = jnp.concatenate(last[keep:], axis=0)

    @pl.when(is_l2)
    def _():
        conv_tile(True)

    @pl.when(is_conv & jnp.logical_not(is_l2))
    def _():
        conv_tile(False)


def _inproj(x2, mod, nw, w_t, conv_big, seq, tm, tn, conv_tiles, l2_tiles, segments):
    t, d = x2.shape
    n = conv_big.shape[1]
    per_seq = seq // tm
    sub = min(256, tm)

    def w_row(j):
        units, prev = j * (tn // BF16_SUBLANES), 0
        for p_col, src_col in segments[1:]:
            delta = src_col - p_col
            assert p_col % tn == 0 and (delta - prev) % BF16_SUBLANES == 0
            units = units + (j >= p_col // tn).astype(jnp.int32) * ((delta - prev) // BF16_SUBLANES)
            prev = delta
        return units * BF16_SUBLANES

    return pl.pallas_call(
        functools.partial(_inproj_kernel, d=d, tm=tm, sub=sub, per_seq=per_seq, conv_tiles=conv_tiles,
                          l2_tiles=l2_tiles),
        grid=(t // tm, n // tn),
        in_specs=[pl.BlockSpec((tm, d), lambda i, j: (i, 0)),
                  pl.BlockSpec((None, 1, 3 * d), lambda i, j: (i // per_seq, 0, 0)),
                  pl.BlockSpec((1, d), lambda i, j: (0, 0)),
                  pl.BlockSpec((pl.Element(tn), pl.Element(d)), lambda i, j: (w_row(j), 0)),
                  pl.BlockSpec((8, tn), lambda i, j: (0, j))],
        out_specs=pl.BlockSpec((tm, tn), lambda i, j: (i, j)),
        out_shape=jax.ShapeDtypeStruct((t, n), BF16),
        scratch_shapes=[pltpu.VMEM((tm, d), BF16), pltpu.VMEM((tm, d), BF16),
                        pltpu.VMEM((n // tn, (CONV_WIDTH - 1) * F32_SUBLANES, tn), F32),
                        pltpu.VMEM((tn // LANES, sub, LANES), F32), pltpu.VMEM((d // LANES, CONV_BLOCK, LANES), F32)],
        compiler_params=_cparams(2),
    )(x2, mod, nw, w_t, conv_big)


def _same_block(ri, ci, size):
    shift = size.bit_length() - 1
    return (ri >> shift) == (ci >> shift)


def _gates_kernel(x_ref, mod_ref, nw_ref, whi_ref, wlo_ref, pg_ref, pm_ref,
                  gcol_ref, grow_ref, mcol_ref, mrow_ref, mstate_ref, *, d, tg, per_seq):
    h = _prenorm(x_ref[...], mod_ref, nw_ref, d)
    h_hi = h.astype(BF16)
    h_lo = (h - h_hi.astype(F32)).astype(BF16)
    w_hi, w_lo = whi_ref[...], wlo_ref[...]
    y = (jnp.dot(h_hi, w_hi, preferred_element_type=F32)
         + (jnp.dot(h_hi, w_lo, preferred_element_type=F32) + jnp.dot(h_lo, w_hi, preferred_element_type=F32)))
    yg, ym = y[:, :GATE_LANES], y[:, GATE_LANES:]
    lane = lax.broadcasted_iota(jnp.int32, (tg, GATE_LANES), 1)

    slot = lane & (GDN_GATE_SLOTS - 1)
    g = -jnp.exp(pg_ref[0:1, :]) * _softplus(yg + pg_ref[1:2, :])
    is_g = (slot == 0) | (slot == 1) | (slot == 4) | (slot == 5)
    is_beta = (slot == 2) | (slot == 3)
    y_g = jnp.where(is_g, g, jnp.where(is_beta, _sigmoid(yg), 0.0))

    mslot = lane & (MLSTM_GATE_SLOTS - 1)
    pre = ym + pm_ref[0:1, :]
    y_m = jnp.where(mslot == 0, pre, jnp.where((mslot == 1) | (mslot == 2), -_softplus(-pre), 0.0))

    ri = lax.broadcasted_iota(jnp.int32, (tg, tg), 0)
    ci = lax.broadcasted_iota(jnp.int32, (tg, tg), 1)
    same = _same_block(ri, ci, CHUNK)
    lower = jnp.where(same & (ri >= ci), 1.0, 0.0).astype(BF16)
    total = jnp.where(same, 1.0, 0.0).astype(BF16)
    y_all = jnp.concatenate([y_g, y_m], axis=1)
    parts = []
    rest = y_all
    for _ in range(3):
        piece = rest.astype(BF16)
        parts.append(piece)
        rest = rest - piece.astype(F32)
    cum = sum(jnp.dot(lower, piece, preferred_element_type=F32) for piece in reversed(parts))
    tot = sum(jnp.dot(total, piece, preferred_element_type=F32) for piece in reversed(parts))

    g_out = jnp.where(slot < 2, cum[:, :GATE_LANES],
                      jnp.where((slot == 4) | (slot == 5), tot[:, :GATE_LANES], y_g))
    m_out = jnp.where(mslot == 1, cum[:, GATE_LANES:],
                      jnp.where(mslot == 2, tot[:, GATE_LANES:], y_m))

    @pl.when(pl.program_id(0) % per_seq == 0)
    def _():
        mstate_ref[...] = jnp.zeros_like(mstate_ref)

    b_al = pltpu.roll(m_out, GATE_LANES - 1, axis=1)
    bl_al = pltpu.roll(m_out, GATE_LANES - 2, axis=1)
    log_end = bl_al - b_al + m_out
    m = mstate_ref[0:1, :]
    m_starts, m_nexts = [], []
    for c in range(tg // CHUNK):
        r0 = c * CHUNK
        m_starts.append(jnp.broadcast_to(m, (CHUNK, GATE_LANES)))
        m = jnp.maximum(bl_al[r0:r0 + 1, :] + m, jnp.max(log_end[r0:r0 + CHUNK, :], axis=0, keepdims=True))
        m_nexts.append(jnp.broadcast_to(m, (CHUNK, GATE_LANES)))
    mstate_ref[...] = jnp.broadcast_to(m, mstate_ref.shape)
    m_out = jnp.where(mslot == 3, pltpu.roll(jnp.concatenate(m_starts, axis=0), 3, axis=1),
                      jnp.where(mslot == 4, pltpu.roll(jnp.concatenate(m_nexts, axis=0), 4, axis=1), m_out))
    gcol_ref[...] = g_out
    grow_ref[...] = g_out.T
    mcol_ref[...] = m_out
    mrow_ref[...] = m_out.T


def _gates(x2, mod, nw, w_gates, pg, pm, batch, seq, tg):
    t, d = x2.shape
    per_seq = seq // tg
    w_hi = w_gates.astype(BF16)
    w_lo = (w_gates - w_hi.astype(F32)).astype(BF16)
    col = jax.ShapeDtypeStruct((t, GATE_LANES), F32)
    row = jax.ShapeDtypeStruct((batch, GATE_LANES, seq), F32)
    col_spec = pl.BlockSpec((tg, GATE_LANES), lambda i: (i, 0))
    row_spec = pl.BlockSpec((None, GATE_LANES, tg), lambda i: (i // per_seq, 0, i % per_seq))
    const = lambda shape: pl.BlockSpec(shape, lambda i: (0, 0))
    return pl.pallas_call(
        functools.partial(_gates_kernel, d=d, tg=tg, per_seq=per_seq),
        grid=(t // tg,),
        in_specs=[pl.BlockSpec((tg, d), lambda i: (i, 0)),
                  pl.BlockSpec((None, 1, 3 * d), lambda i: (i // per_seq, 0, 0)),
                  const((1, d)), const((d, 2 * GATE_LANES)), const((d, 2 * GATE_LANES)),
                  const((2, GATE_LANES)), const((1, GATE_LANES))],
        out_specs=[col_spec, row_spec, col_spec, row_spec],
        out_shape=[col, row, col, row],
        scratch_shapes=[pltpu.VMEM((8, GATE_LANES), F32)],
        compiler_params=_cparams(1),
    )(x2, mod, nw, w_hi, w_lo, pg, pm)


def _unit_lower_inverses(mats, ri, ci):
    diag = _same_block(ri, ci, INV_BASE)
    eye = jnp.where(ri == ci, 1.0, 0.0)
    pows = [jnp.where(diag, a, 0.0) for a in mats]
    invs = [eye - ab for ab in pows]
    order = 2
    while order < INV_BASE:
        pows = [_mm(ab, ab) for ab in pows]
        yield
        invs = [inv + _mm(inv, ab) for inv, ab in zip(invs, pows)]
        yield
        order *= 2
    size = INV_BASE
    while size < CHUNK:
        merged = _same_block(ri, ci, 2 * size)
        keep = merged & jnp.logical_not(diag)
        corr = [_mm(_odd_blocks(jnp.where(keep, a, 0.0), size), inv) for a, inv in zip(mats, invs)]
        yield
        zeros = jnp.zeros_like(invs[0])
        invs = [_with_odd_blocks(inv, _odd_blocks(inv, size) - _mm(_odd_blocks(inv, size),
                                                                     _with_odd_blocks(zeros, cr, size)), size)
                for inv, cr in zip(invs, corr)]
        yield
        diag, size = merged, 2 * size
    return invs


def _odd_blocks(x, size):
    return jnp.concatenate([x[b * size:(b + 1) * size] for b in range(1, x.shape[0] // size, 2)], axis=0)


def _with_odd_blocks(x, odd, size):
    blocks = [odd[(b // 2) * size:(b // 2 + 1) * size] if b % 2 else x[b * size:(b + 1) * size]
              for b in range(x.shape[0] // size)]
    return jnp.concatenate(blocks, axis=0)


def _interleave(*gens):
    live = list(gens)
    while live:
        for gen in list(live):
            try:
                next(gen)
            except StopIteration:
                live.remove(gen)


def _lane_slot_shift(group, slots):
    return lax.rem((GATE_LANES // slots - group) * slots, GATE_LANES)


def _group_masks():
    ri = lax.broadcasted_iota(jnp.int32, (GROUP, GROUP), 0)
    ci = lax.broadcasted_iota(jnp.int32, (GROUP, GROUP), 1)
    same = _same_block(ri, ci, CHUNK)
    return ri, ci, same & (ri >= ci), same & (ri > ci)


def _rows(x, g):
    return x[g * GROUP:(g + 1) * GROUP]


def _lanes(x, g):
    return x[:, g * GROUP:(g + 1) * GROUP]


def _gdn_kernel(q_ref, k_ref, v_ref, z_ref, gcol_ref, grow_ref, nw_ref, o_ref, s_ref, *, tt, hps):
    hd = GDN_HEAD_DIM
    n_groups, n_chunks, per_group = tt // GROUP, tt // CHUNK, GROUP // CHUNK

    @pl.when(pl.program_id(2) == 0)
    def _():
        s_ref[...] = jnp.zeros_like(s_ref)

    gcol = pltpu.roll(gcol_ref[...], _lane_slot_shift(pl.program_id(1) * hps, GDN_GATE_SLOTS), axis=1)
    ri, ci, causal, strict = _group_masks()
    n_waves = hps // GDN_WAVE
    waves = [dict(pairs=range(w * GDN_WAVE, (w + 1) * GDN_WAVE), heads=[], a_mats=[], done=[]) for w in range(n_waves)]

    def prep(w):
        return _gdn_prep(waves[w], q_ref, k_ref, v_ref, gcol, grow_ref, causal, strict, n_groups)

    def finish(w):
        for vh, state, out in waves[w]["done"]:
            s_ref[vh] = state
            o = _rms(out) * nw_ref[...]
            z = z_ref[:, vh * hd:(vh + 1) * hd].astype(F32)
            o_ref[:, vh * hd:(vh + 1) * hd] = (o * _silu(z)).astype(BF16)
            yield

    def wy(w):
        return _gdn_wy(waves[w], ri, ci, n_groups)

    def scan(w):
        return _gdn_scan(waves[w], s_ref, n_chunks, per_group)

    _interleave(prep(0))
    for w in range(n_waves + 1):
        gens = ([wy(w)] if w < n_waves else []) + ([prep(w + 1)] if w + 1 < n_waves else [])
        gens += ([scan(w - 1)] if w >= 1 else []) + ([finish(w - 2)] if w >= 2 else [])
        _interleave(*gens)
    _interleave(finish(n_waves - 1))


def _gdn_prep(wave, q_ref, k_ref, v_ref, gcol, grow_ref, causal, strict, n_groups):
    hd = GDN_HEAD_DIM
    heads, a_mats = wave["heads"], wave["a_mats"]
    for pp in wave["pairs"]:
        q16, k16 = q_ref[:, pp * hd:(pp + 1) * hd], k_ref[:, pp * hd:(pp + 1) * hd]
        q, k_t = q16.astype(F32), k16.astype(F32).T
        kk = [_mm_nt(_rows(k16, g), _rows(k16, g)) for g in range(n_groups)]
        qk = [_mm_nt(_rows(q16, g), _rows(k16, g)) for g in range(n_groups)]
        yield
        for hh in range(2):
            lane = pp * GDN_GATE_SLOTS + hh
            gc_c, be_c = gcol[:, lane:lane + 1], gcol[:, lane + 2:lane + 3]
            gc_r, be_r, gl_r = grow_ref[pp, hh:hh + 1, :], grow_ref[pp, 2 + hh:3 + hh, :], grow_ref[pp, 4 + hh:5 + hh, :]
            decay = [jnp.exp(jnp.where(causal, _rows(gc_c, g) - _lanes(gc_r, g), -jnp.inf)) for g in range(n_groups)]
            a_mats += [jnp.where(strict, _rows(be_c, g) * kk[g] * decay[g], 0.0) for g in range(n_groups)]
            vh = 2 * pp + hh
            heads.append(dict(vh=vh, k16=k16, qk=qk, decay=decay, t_scale=be_r, w_scale=be_r * jnp.exp(gc_r),
                              qd=q * jnp.exp(gc_c), kt_t=(k_t * jnp.exp(gl_r - gc_r)).astype(BF16),
                              g_tot=jnp.exp(gl_r), v=v_ref[:, vh * hd:(vh + 1) * hd]))
            yield


def _gdn_wy(wave, ri, ci, n_groups):
    heads = wave["heads"]
    invs = yield from _unit_lower_inverses(wave["a_mats"], ri, ci)
    for hi, h in enumerate(heads):
        h["w"], h["u"], h["attn"] = [], [], []
        for g in range(n_groups):
            inv = invs[hi * n_groups + g]
            h["w"].append(_mm(inv * _lanes(h["w_scale"], g), _rows(h["k16"], g)))
            h["u"].append(_mm(inv * _lanes(h["t_scale"], g), _rows(h["v"], g)))
            h["attn"].append((h["qk"][g] * h["decay"][g]).astype(BF16))
        yield


def _gdn_scan(wave, s_ref, n_chunks, per_group):
    hd = GDN_HEAD_DIM
    heads = wave["heads"]
    states = [s_ref[h["vh"]] for h in heads]
    outs = [[] for _ in heads]
    zeros = jnp.zeros((CHUNK, hd), F32)
    for c in range(n_chunks):
        g, pos = c // per_group, c % per_group
        r0, rows = pos * CHUNK, slice(c * CHUNK, (c + 1) * CHUNK)
        prods = [_mm(jnp.concatenate([h["w"][g][r0:r0 + CHUNK], h["qd"][rows]], axis=0), states[hi])
                 for hi, h in enumerate(heads)]
        yield
        for hi, h in enumerate(heads):
            v_new = h["u"][g][r0:r0 + CHUNK] - prods[hi][0:CHUNK]
            v_group = jnp.concatenate([v_new if b == pos else zeros for b in range(per_group)], axis=0)
            outs[hi].append(prods[hi][CHUNK:2 * CHUNK] + _mm(h["attn"][g][r0:r0 + CHUNK], v_group))
            states[hi] = states[hi] * h["g_tot"][:, c * CHUNK:c * CHUNK + 1] + _mm(h["kt_t"][:, rows], v_new)
        yield
    wave["done"].extend((h["vh"], states[hi], jnp.concatenate(outs[hi], axis=0)) for hi, h in enumerate(heads))


def _gdn(p,gcol, grow, norm_w, batch, seq, tt, hps, col_q, col_k, col_v, col_z):
    t = batch * seq
    hd = GDN_HEAD_DIM
    qw, vw = hps * hd, 2 * hps * hd
    per_seq = seq // tt
    rowi = lambda b, h, i: b * per_seq + i
    return pl.pallas_call(
        functools.partial(_gdn_kernel, tt=tt, hps=hps),
        grid=(batch, GDN_QK_HEADS // hps, per_seq),
        in_specs=[
            pl.BlockSpec((tt, qw), lambda b, h, i: (rowi(b, h, i), col_q // qw + h)),
            pl.BlockSpec((tt, qw), lambda b, h, i: (rowi(b, h, i), col_k // qw + h)),
            pl.BlockSpec((tt, vw), lambda b, h, i: (rowi(b, h, i), col_v // vw + h)),
            pl.BlockSpec((tt, vw), lambda b, h, i: (rowi(b, h, i), col_z // vw + h)),
            pl.BlockSpec((tt, GATE_LANES), lambda b, h, i: (rowi(b, h, i), 0)),
            pl.BlockSpec((None, hps, GDN_GATE_SLOTS, tt), lambda b, h, i: (b, h, 0, i)),
            pl.BlockSpec((1, hd), lambda b, h, i: (0, 0)),
        ],
        out_specs=pl.BlockSpec((tt, vw), lambda b, h, i: (rowi(b, h, i), h)),
        out_shape=jax.ShapeDtypeStruct((t, GDN_V_HEADS * hd), BF16),
        scratch_shapes=[pltpu.VMEM((2 * hps, hd, hd), F32)],
        compiler_params=_cparams(3),
    )(p, p, p, p, gcol, grow, norm_w)


def _mlstm_kernel(q_ref, k_ref, v_ref, og_ref, z_ref, mcol_ref, mrow_ref, nw_ref, o_ref, c_ref, *, tt, hps):
    dk, dv = MLSTM_QK_DIM, MLSTM_V_DIM
    q_scale = dk ** -0.5
    n_groups, n_chunks = tt // GROUP, tt // CHUNK

    @pl.when(pl.program_id(2) == 0)
    def _():
        c_ref[...] = jnp.zeros_like(c_ref)

    mcol = pltpu.roll(mcol_ref[...], _lane_slot_shift(pl.program_id(1) * hps, MLSTM_GATE_SLOTS), axis=1)
    _, _, causal, _ = _group_masks()
    lane = lax.broadcasted_iota(jnp.int32, (tt, 128), 1)
    ones_col = jnp.where(lane == 0, 1.0, 0.0).astype(BF16)

    def mix(wave_heads, heads):
        for hh in wave_heads:
            q16, k16 = q_ref[:, hh * dk:(hh + 1) * dk], k_ref[:, hh * dk:(hh + 1) * dk]
            v_aug = jnp.concatenate([v_ref[:, hh * dv:(hh + 1) * dv], ones_col], axis=1)
            lane0 = hh * MLSTM_GATE_SLOTS
            b_c, m_start_c = mcol[:, lane0 + 1:lane0 + 2], mcol[:, lane0 + 3:lane0 + 4]
            i_r, b_r, bl_r = mrow_ref[hh, 0:1, :], mrow_ref[hh, 1:2, :], mrow_ref[hh, 2:3, :]
            m_start_r, m_next_r = mrow_ref[hh, 3:4, :], mrow_ref[hh, 4:5, :]
            log_end = bl_r - b_r + i_r
            k_ts = (k16.astype(F32).T * jnp.exp(log_end - m_next_r)).astype(BF16)
            heads.append(dict(hh=hh, q16=q16, k16=k16, v_aug=v_aug, b_c=b_c, b_r=b_r, i_r=i_r,
                              m_inter=b_c + m_start_c, k_ts=k_ts, carry=jnp.exp(bl_r + m_start_r - m_next_r)))
            yield
        probs = [(h, g) for h in heads for g in range(n_groups)]
        qks = [_mm_nt(_rows(h["q16"], g), _rows(h["k16"], g)) for h, g in probs]
        yield
        log_ds = [jnp.where(causal, _rows(h["b_c"], g) - _lanes(h["b_r"], g) + _lanes(h["i_r"], g), -jnp.inf)
                  for h, g in probs]
        yield
        row_max = [jnp.max(ld, axis=1, keepdims=True) for ld in log_ds]
        yield
        m_ts = [jnp.maximum(_rows(h["m_inter"], g), rm) for (h, g), rm in zip(probs, row_max)]
        weights = [qk * (q_scale * jnp.exp(ld - m_t)) for qk, ld, m_t in zip(qks, log_ds, m_ts)]
        yield
        intras = [_mm(s, _rows(h["v_aug"], g)) for (h, g), s in zip(probs, weights)]
        yield
        for hi, h in enumerate(heads):
            h["m_t"] = jnp.concatenate(m_ts[hi * n_groups:(hi + 1) * n_groups], axis=0)
            h["intra"] = jnp.concatenate(intras[hi * n_groups:(hi + 1) * n_groups], axis=0)
        for h in heads:
            h["kv"] = [_mm(h["k_ts"][:, c * CHUNK:(c + 1) * CHUNK], h["v_aug"][c * CHUNK:(c + 1) * CHUNK])
                       for c in range(n_chunks)]
            yield
        for h in heads:
            state = c_ref[h["hh"]]
            inters = []
            for c in range(n_chunks):
                inters.append(_mm(h["q16"][c * CHUNK:(c + 1) * CHUNK], state))
                state = state * h["carry"][:, c * CHUNK:c * CHUNK + 1] + h["kv"][c]
            c_ref[h["hh"]] = state
            h["inter"] = jnp.concatenate(inters, axis=0)
            yield

    def out(heads):
        for h in heads:
            m_t = h["m_t"]
            num = (q_scale * jnp.exp(h["m_inter"] - m_t)) * h["inter"] + h["intra"]
            den = jnp.maximum(jnp.abs(num[:, dv:dv + 1]), jnp.exp(-m_t))
            cols = slice(h["hh"] * dv, (h["hh"] + 1) * dv)
            hn = _rms(num[:, 0:dv] / den) * nw_ref[:, cols]
            yield
            o_ref[:, cols] = (_sigmoid(og_ref[:, cols].astype(F32)) * hn
                              * _silu(z_ref[:, cols].astype(F32))).astype(BF16)
            yield

    waves = [list(range(w, min(w + MLSTM_WAVE, hps))) for w in range(0, hps, MLSTM_WAVE)]
    done = [[] for _ in waves]
    for w in range(len(waves) + 1):
        _interleave(*([mix(waves[w], done[w])] if w < len(waves) else []), *([out(done[w - 1])] if w >= 1 else []))


def _mlstm(p, mcol, mrow, norm_w, batch, seq, tt, hps, col_q, col_k, col_v, col_o, col_z):
    t = batch * seq
    dk, dv = MLSTM_QK_DIM, MLSTM_V_DIM
    qw, vw = hps * dk, hps * dv
    per_seq = seq // tt
    rowi = lambda b, h, i: b * per_seq + i
    return pl.pallas_call(
        functools.partial(_mlstm_kernel, tt=tt, hps=hps),
        grid=(batch, MLSTM_HEADS // hps, per_seq),
        in_specs=[
            pl.BlockSpec((tt, qw), lambda b, h, i: (rowi(b, h, i), col_q // qw + h)),
            pl.BlockSpec((tt, qw), lambda b, h, i: (rowi(b, h, i), col_k // qw + h)),
            pl.BlockSpec((tt, vw), lambda b, h, i: (rowi(b, h, i), col_v // vw + h)),
            pl.BlockSpec((tt, vw), lambda b, h, i: (rowi(b, h, i), col_o // vw + h)),
            pl.BlockSpec((tt, vw), lambda b, h, i: (rowi(b, h, i), col_z // vw + h)),
            pl.BlockSpec((tt, GATE_LANES), lambda b, h, i: (rowi(b, h, i), 0)),
            pl.BlockSpec((None, hps, MLSTM_GATE_SLOTS, tt), lambda b, h, i: (b, h, 0, i)),
            pl.BlockSpec((1, vw), lambda b, h, i: (0, h)),
        ],
        out_specs=pl.BlockSpec((tt, vw), lambda b, h, i: (rowi(b, h, i), h)),
        out_shape=jax.ShapeDtypeStruct((t, MLSTM_HEADS * dv), BF16),
        scratch_shapes=[pltpu.VMEM((hps, dk, dv + 128), F32)],
        compiler_params=_cparams(3),
    )(p, p, p, p, p, mcol, mrow, norm_w)


def _merge_kernel(ya_ref, hb_ref, wa_ref, wb_ref, ga_ref, gb_ref, o_ref):
    ya = jnp.dot(ya_ref[...], wa_ref[...], preferred_element_type=F32)
    yb = jnp.dot(hb_ref[...], wb_ref[...], preferred_element_type=F32)
    merged = _sigmoid(ga_ref[...].astype(F32)) * ya + _sigmoid(gb_ref[...].astype(F32)) * yb
    o_ref[...] = merged.astype(BF16)


def _merge(ya, hb, wa, wb, p, col_ga, col_gb, tm, tn):
    t = ya.shape[0]
    d = wa.shape[1]
    return pl.pallas_call(
        _merge_kernel,
        grid=(t // tm, d // tn),
        in_specs=[pl.BlockSpec((tm, ya.shape[1]), lambda i, j: (i, 0)),
                  pl.BlockSpec((tm, hb.shape[1]), lambda i, j: (i, 0)),
                  pl.BlockSpec((wa.shape[0], tn), lambda i, j: (0, j)),
                  pl.BlockSpec((wb.shape[0], tn), lambda i, j: (0, j)),
                  pl.BlockSpec((tm, tn), lambda i, j: (i, col_ga // tn + j)),
                  pl.BlockSpec((tm, tn), lambda i, j: (i, col_gb // tn + j))],
        out_specs=pl.BlockSpec((tm, tn), lambda i, j: (i, j)),
        out_shape=jax.ShapeDtypeStruct((t, d), BF16),
        compiler_params=_cparams(2),
    )(ya, hb, wa, wb, p, p)


def _out_kernel(mg_ref, w_ref, x_ref, mod_ref, nw_ref, o_ref, *, d):
    out = jnp.dot(mg_ref[...], w_ref[...], preferred_element_type=F32)
    o_ref[...] = x_ref[...] + mod_ref[:, 2 * d:3 * d] * (_rms(out) * nw_ref[...])


def _out(mg, w_out, x2, mod, nw, seq, tm):
    t, d = x2.shape
    per_seq = seq // tm
    return pl.pallas_call(
        functools.partial(_out_kernel, d=d),
        grid=(t // tm,),
        in_specs=[pl.BlockSpec((tm, d), lambda i: (i, 0)),
                  pl.BlockSpec((d, d), lambda i: (0, 0)),
                  pl.BlockSpec((tm, d), lambda i: (i, 0)),
                  pl.BlockSpec((None, 1, 3 * d), lambda i: (i // per_seq, 0, 0)),
                  pl.BlockSpec((1, d), lambda i: (0, 0))],
        out_specs=pl.BlockSpec((tm, d), lambda i: (i, 0)),
        out_shape=jax.ShapeDtypeStruct((t, d), F32),
        compiler_params=_cparams(1),
    )(mg, w_out, x2, mod, nw)


def _gate_columns(d):
    qkv = 2 * GDN_QK_HEADS * GDN_HEAD_DIM + GDN_V_HEADS * GDN_HEAD_DIM
    col_a, col_b = qkv, qkv + GDN_V_HEADS
    col_i = col_b + GDN_V_HEADS + GDN_V_HEADS * GDN_HEAD_DIM + 2 * MLSTM_HEADS * MLSTM_QK_DIM + MLSTM_HEADS * MLSTM_V_DIM
    col_f = col_i + MLSTM_HEADS
    g_src, g_head, g_use = np.zeros(GATE_LANES, np.int32), np.zeros(GATE_LANES, np.int32), np.zeros(GATE_LANES, bool)
    m_src, m_head, m_use = np.zeros(GATE_LANES, np.int32), np.zeros(GATE_LANES, np.int32), np.zeros(GATE_LANES, bool)
    for lane in range(GATE_LANES):
        pair, slot = divmod(lane, GDN_GATE_SLOTS)
        if slot < 6:
            head = 2 * pair + slot % 2
            g_src[lane] = (col_b if slot in (2, 3) else col_a) + head
            g_head[lane], g_use[lane] = head, True
        head, slot = divmod(lane, MLSTM_GATE_SLOTS)
        if slot < 3:
            m_src[lane] = (col_i if slot == 0 else col_f) + head
            m_head[lane], m_use[lane] = head, True
    slot_is_i = (np.arange(GATE_LANES) % MLSTM_GATE_SLOTS) == 0
    return (g_src, g_head, g_use), (m_src, m_head, m_use), slot_is_i


def _layer(x, c, w_ada, b_ada, norm_pre_w, w_in, gdn_conv_w, gdn_A_log, gdn_dt_bias, gdn_norm_w,
           mlstm_conv_w, mlstm_b_i, mlstm_b_f, mlstm_norm_w, w_proj_gdn, w_proj_mlstm, w_out, norm_post_w):
    batch, seq, d = x.shape
    t = batch * seq
    qk_w = GDN_QK_HEADS * GDN_HEAD_DIM
    v_w = GDN_V_HEADS * GDN_HEAD_DIM
    mqk_w = MLSTM_HEADS * MLSTM_QK_DIM
    mv_w = MLSTM_HEADS * MLSTM_V_DIM
    assert d == qk_w == mv_w and seq % 256 == 0

    src_a = 2 * qk_w + v_w
    src_z = src_a + 2 * GDN_V_HEADS
    src_mqk = src_z + v_w
    src_i = src_mqk + 2 * mqk_w + mv_w
    src_o = src_i + 2 * MLSTM_HEADS
    col_q, col_k, col_v, col_z = 0, qk_w, 2 * qk_w, 2 * qk_w + v_w
    col_mq = col_z + v_w
    col_mk, col_mv = col_mq + mqk_w, col_mq + 2 * mqk_w
    col_mo = col_mv + mv_w
    col_mz, col_ga, col_gb = col_mo + mv_w, col_mo + 2 * mv_w, col_mo + 2 * mv_w + d
    n_p = col_gb + d
    tn = 1024
    segments = ((0, 0), (col_z, src_z), (col_mo, src_o))
    assert all(p_col % tn == 0 for p_col, _ in segments) and n_p - col_mo == w_in.shape[1] - src_o
    w_t = jnp.swapaxes(w_in, 0, 1)

    conv_big = jnp.zeros((8, n_p), F32).at[0:CONV_WIDTH, col_q:col_z].set(0.5 * gdn_conv_w)
    conv_big = conv_big.at[0:CONV_WIDTH, col_mq:col_mv].set(0.5 * mlstm_conv_w)
    assert col_z % tn == 0 and col_mq % tn == 0 and col_mv % tn == 0
    conv_tiles = ((col_q // tn, col_z // tn), (col_mq // tn, col_mv // tn))
    assert col_q == 0 and col_k % tn == 0 and col_v == 2 * col_k and GDN_HEAD_DIM == LANES
    l2_tiles = col_v // tn

    (g_src, g_head, g_use), (m_src, m_head, m_use), slot_is_i = _gate_columns(d)
    wg = jnp.where(g_use[None, :], jnp.take(w_t, g_src, axis=0).T, 0.0)
    wm = jnp.where(m_use[None, :], jnp.take(w_t, m_src, axis=0).T, 0.0)
    pg = jnp.stack([jnp.where(g_use, jnp.take(gdn_A_log, g_head), 0.0),
                    jnp.where(g_use, jnp.take(gdn_dt_bias, g_head), 0.0)])
    pm = jnp.where(m_use, jnp.where(slot_is_i, jnp.take(mlstm_b_i, m_head), jnp.take(mlstm_b_f, m_head)), 0.0)[None, :]

    x2 = x.reshape(t, d)
    rows = max(8, batch)
    c8 = jnp.pad(c, ((0, rows - batch), (0, 0)))
    mod = _ada(c8, w_ada, b_ada)[:batch].reshape(batch, 1, 3 * d)
    nw_pre = norm_pre_w.reshape(1, d)

    tm = min(1024, seq)
    p = _inproj(x2, mod, nw_pre, w_t, conv_big, seq, tm=tm, tn=tn, conv_tiles=conv_tiles, l2_tiles=l2_tiles,
                segments=segments)
    gcol, grow, mcol, mrow = _gates(x2, mod, nw_pre, jnp.concatenate([wg, wm], axis=1), pg, pm, batch, seq,
                                    tg=min(512, seq))
    grow = grow.reshape(batch, GDN_QK_HEADS, GDN_GATE_SLOTS, seq)
    mrow = mrow.reshape(batch, MLSTM_HEADS, MLSTM_GATE_SLOTS, seq)

    tt = 256
    ya = _gdn(p, gcol, grow, gdn_norm_w.reshape(1, GDN_HEAD_DIM), batch, seq, tt, 16,
              col_q, col_k, col_v, col_z)
    hb = _mlstm(p, mcol, mrow, mlstm_norm_w.reshape(1, mv_w), batch, seq, tt, 8,
                col_mq, col_mk, col_mv, col_mo, col_mz)
    mg = _merge(ya, hb, w_proj_gdn.astype(BF16), w_proj_mlstm.astype(BF16), p, col_ga, col_gb, tm=tm, tn=512)
    y = _out(mg, w_out.astype(BF16), x2, mod, norm_post_w.reshape(1, d), seq, tm=min(512, seq))
    return y.reshape(batch, seq, d)


def kernel(x, c, w_ada, b_ada, norm_pre_w, w_in, gdn_conv_w, gdn_A_log, gdn_dt_bias, gdn_norm_w, mlstm_conv_w, mlstm_b_i, mlstm_b_f, mlstm_norm_w, w_proj_gdn, w_proj_mlstm, w_out, norm_post_w):
    for l in range(w_ada.shape[0]):
        x = _layer(x, c, w_ada[l], b_ada[l], norm_pre_w[l], w_in[l], gdn_conv_w[l], gdn_A_log[l],
                   gdn_dt_bias[l], gdn_norm_w[l], mlstm_conv_w[l], mlstm_b_i[l], mlstm_b_f[l],
                   mlstm_norm_w[l], w_proj_gdn[l], w_proj_mlstm[l], w_out[l], norm_post_w[l])
    return x
```
